```python
import math
import jax, jax.numpy as jnp
from jax import lax
import numpy as np

D_MODEL = 4096
BATCH = 4
SEQ = 2048
DEPTH = 4
DEC_BATCH = 32
DEC_SEQ = 4
PAST_LEN = 8192
PAGE_SIZE = 128

D_MIX = D_MODEL
HEAD_DIM = 128
D_ATTN = D_MIX // 2
D_CONV = D_MIX // 4
D_POOL = D_MIX - D_ATTN - D_CONV
N_HEADS = D_ATTN // HEAD_DIM
N_KV_HEADS = max(1, N_HEADS // 4)
Q_PER_KV = N_HEADS // N_KV_HEADS
D_KV = N_KV_HEADS * HEAD_DIM
ROT_DIM = HEAD_DIM // 4
ROPE_THETA = 500000.0
WINDOW = 128
ATTN_BLOCK = 128
KV_BUF = min(WINDOW, PAST_LEN)
CONV_WIDTH = 31
CONV_BUF = CONV_WIDTH - 1
POOL_WINDOWS = (2, 4, 8, 16)
N_POOL_GROUPS = len(POOL_WINDOWS)
POOL_GROUP = D_POOL // N_POOL_GROUPS
POOL_BUF = max(POOL_WINDOWS) - 1
D_FF = 4 * D_MODEL
D_IN = 2 * D_CONV + D_POOL + D_ATTN + 2 * D_KV
EPS = 1e-6

kernel_name = "hymba_conv_pool_swa_decoder_step"


def rms_norm(x, g):
    xf = x.astype(jnp.float32)
    y = xf * lax.rsqrt(jnp.mean(xf * xf, axis=-1, keepdims=True) + EPS)
    return (y * g.astype(jnp.float32)).astype(x.dtype)


def layer_norm(x, g, b):
    xf = x.astype(jnp.float32)
    xc = xf - jnp.mean(xf, axis=-1, keepdims=True)
    y = xc * lax.rsqrt(jnp.mean(xc * xc, axis=-1, keepdims=True) + EPS)
    return (y * g.astype(jnp.float32) + b.astype(jnp.float32)).astype(x.dtype)


def partial_rope(x, pos):
    half = ROT_DIM // 2
    inv_freq = jnp.exp(-math.log(ROPE_THETA) * 2.0 * jnp.arange(half, dtype=jnp.float32) / ROT_DIM)
    ang = pos.astype(jnp.float32)[:, None] * inv_freq[None, :]
    cos = jnp.cos(ang)[:, None, :]
    sin = jnp.sin(ang)[:, None, :]
    xf = x.astype(jnp.float32)
    x1 = xf[..., :half]
    x2 = xf[..., half:ROT_DIM]
    out = jnp.concatenate([x1 * cos - x2 * sin, x2 * cos + x1 * sin, xf[..., ROT_DIM:]], axis=-1)
    return out.astype(x.dtype)


def causal_depthwise_conv(u, hist, w, b):
    full = jnp.concatenate([hist.astype(u.dtype), u], axis=1)
    out = lax.conv_general_dilated(
        full, w[:, None, :].astype(u.dtype), window_strides=(1,), padding="VALID",
        dimension_numbers=("NWC", "WIO", "NWC"), feature_group_count=D_CONV)
    return out + b.astype(out.dtype), full[:, -CONV_BUF:]


def multiscale_pool(z, hist, pos, w_pool, scale):
    B, S, _ = z.shape
    full = jnp.concatenate([hist.astype(z.dtype), z], axis=1)
    ff = full.astype(jnp.float32)
    cs = jnp.concatenate([jnp.zeros_like(ff[:, :1]), jnp.cumsum(ff, axis=1)], axis=1)
    end = cs[:, POOL_BUF + 1:]
    means = []
    for g, w in enumerate(POOL_WINDOWS):
        c0, c1 = g * POOL_GROUP, (g + 1) * POOL_GROUP
        start = POOL_BUF + 1 - w
        s = end[..., c0:c1] - cs[:, start:start + S, c0:c1]
        cnt = jnp.minimum(pos + 1, w).astype(jnp.float32)[None, :, None]
        means.append(s / cnt)
    mean = jnp.concatenate(means, axis=-1)
    p = (mean - z.astype(jnp.float32)).astype(z.dtype).reshape(B, S, N_POOL_GROUPS, POOL_GROUP)
    y = jnp.einsum("bsgc,gcd->bsgd", p, w_pool).reshape(B, S, D_POOL) * scale.astype(z.dtype)
    return y, full[:, -POOL_BUF:]


def sink_softmax(s, sink):
    sk = jnp.broadcast_to(sink.astype(jnp.float32)[:, :, None, None], s.shape[:-1] + (1,))
    p = jax.nn.softmax(jnp.concatenate([s, sk], axis=-1), axis=-1)
    return p[..., :-1]


def swa_prompt(q, k, v, sinks):
    B, S = q.shape[:2]
    nb = S // ATTN_BLOCK
    qb = q.reshape(B, nb, ATTN_BLOCK, N_KV_HEADS, Q_PER_KV, HEAD_DIM)
    kb = k.reshape(B, nb, ATTN_BLOCK, N_KV_HEADS, HEAD_DIM)
    vb = v.reshape(B, nb, ATTN_BLOCK, N_KV_HEADS, HEAD_DIM)

    def with_prev(t):
        prev = jnp.concatenate([jnp.zeros_like(t[:, :1]), t[:, :-1]], axis=1)
        return jnp.concatenate([prev, t], axis=2)

    kk, vv = with_prev(kb), with_prev(vb)
    s = jnp.einsum("bnqkgd,bnjkd->bnkgqj", qb, kk).astype(jnp.float32) * (HEAD_DIM ** -0.5)
    qi = jnp.arange(ATTN_BLOCK)[:, None]
    kj = jnp.arange(2 * ATTN_BLOCK)[None, :]
    diff = qi + ATTN_BLOCK - kj
    band = (diff >= 0) & (diff <= WINDOW)
    not_first = (jnp.arange(nb) > 0)[:, None, None]
    valid = band[None] & (not_first | (kj >= ATTN_BLOCK)[None])
    s = jnp.where(valid[None, :, None, None], s, -jnp.inf)
    probs = sink_softmax(s, sinks.reshape(N_KV_HEADS, Q_PER_KV))
    o = jnp.einsum("bnkgqj,bnjkd->bnqkgd", probs.astype(vv.dtype), vv)
    return o.reshape(B, S, D_ATTN)


def swa_sample(q, k, v, k_hist, v_hist, sinks):
    B, T = q.shape[:2]
    kk = jnp.concatenate([k_hist.astype(k.dtype), k], axis=1)
    vv = jnp.concatenate([v_hist.astype(v.dtype), v], axis=1)
    qg = q.reshape(B, T, N_KV_HEADS, Q_PER_KV, HEAD_DIM)
    s = jnp.einsum("btkgd,bjkd->bkgtj", qg, kk).astype(jnp.float32) * (HEAD_DIM ** -0.5)
    qpos = PAST_LEN + jnp.arange(T)
    kpos = PAST_LEN - KV_BUF + jnp.arange(KV_BUF + T)
    diff = qpos[:, None] - kpos[None, :]
    valid = (diff >= 0) & (diff <= WINDOW)
    s = jnp.where(valid, s, -jnp.inf)
    probs = sink_softmax(s, sinks.reshape(N_KV_HEADS, Q_PER_KV))
    o = jnp.einsum("bkgtj,bjkd->btkgd", probs.astype(vv.dtype), vv).reshape(B, T, D_ATTN)
    return o, kk[:, -KV_BUF:], vv[:, -KV_BUF:]


def layer_forward(x, pos, conv_hist, pool_hist, kv_hist, p):
    B, S, _ = x.shape
    h = rms_norm(x, p["norm_mix"])
    proj = h @ p["w_in"]
    cuts = [D_CONV, 2 * D_CONV, 2 * D_CONV + D_POOL, 2 * D_CONV + D_POOL + D_ATTN,
            2 * D_CONV + D_POOL + D_ATTN + D_KV]
    a, b, z, q, k, v = jnp.split(proj, cuts, axis=-1)

    u = a * jax.nn.sigmoid(b)
    c, conv_new = causal_depthwise_conv(u, conv_hist, p["w_conv"], p["b_conv"])
    c = jax.nn.silu(layer_norm(c, p["ln_conv_g"], p["ln_conv_b"]))

    y_pool, pool_new = multiscale_pool(z, pool_hist, pos, p["w_pool"], p["pool_scale"])

    q = partial_rope(rms_norm(q.reshape(B, S, N_HEADS, HEAD_DIM), p["q_norm"]), pos)
    k = partial_rope(rms_norm(k.reshape(B, S, N_KV_HEADS, HEAD_DIM), p["k_norm"]), pos)
    v = v.reshape(B, S, N_KV_HEADS, HEAD_DIM)
    if kv_hist is None:
        o_attn = swa_prompt(q, k, v, p["attn_sinks"])
        k_new, v_new = k[:, -KV_BUF:], v[:, -KV_BUF:]
    else:
        o_attn, k_new, v_new = swa_sample(q, k, v, kv_hist[0], kv_hist[1], p["attn_sinks"])

    x = x + jnp.concatenate([c, y_pool, o_attn], axis=-1) @ p["w_out"]
    h2 = rms_norm(x, p["norm_ffn"])
    x = x + jnp.square(jax.nn.relu(h2 @ p["w_up"])) @ p["w_down"]
    return x, conv_new, pool_new, k_new, v_new


def setup_inputs(seed: int = 0) -> dict:
    key = jax.random.key(seed)
    ks = jax.random.split(key, 24)
    f32 = jnp.float32
    n = lambda i, shape: jax.random.normal(ks[i], shape, dtype=f32)
    return {
        "x_prompt": n(0, (BATCH, SEQ, D_MODEL)),
        "x_sample": n(1, (DEC_BATCH, DEC_SEQ, D_MODEL)),
        "state_conv": 0.5 * n(2, (DEPTH, DEC_BATCH, CONV_BUF, D_CONV)),
        "state_pool": n(3, (DEPTH, DEC_BATCH, POOL_BUF, D_POOL)),
        "cache_k": n(4, (DEPTH, DEC_BATCH, KV_BUF, N_KV_HEADS, HEAD_DIM)),
        "cache_v": n(5, (DEPTH, DEC_BATCH, KV_BUF, N_KV_HEADS, HEAD_DIM)),
        "norm_mix": 1.0 + 0.02 * n(6, (DEPTH, D_MODEL)),
        "w_in": n(7, (DEPTH, D_MODEL, D_IN)) * D_MODEL ** -0.5,
        "w_conv": n(8, (DEPTH, CONV_WIDTH, D_CONV)) * CONV_WIDTH ** -0.5,
        "b_conv": 0.02 * n(9, (DEPTH, D_CONV)),
        "ln_conv_g": 1.0 + 0.02 * n(10, (DEPTH, D_CONV)),
        "ln_conv_b": 0.02 * n(11, (DEPTH, D_CONV)),
        "w_pool": n(12, (DEPTH, N_POOL_GROUPS, POOL_GROUP, POOL_GROUP)) * POOL_GROUP ** -0.5,
        "pool_scale": 1.0 + 0.02 * n(13, (DEPTH, D_POOL)),
        "q_norm": 1.0 + 0.02 * n(14, (DEPTH, HEAD_DIM)),
        "k_norm": 1.0 + 0.02 * n(15, (DEPTH, HEAD_DIM)),
        "attn_sinks": 0.5 * n(16, (DEPTH, N_HEADS)),
        "w_out": n(17, (DEPTH, D_MIX, D_MODEL)) * D_MIX ** -0.5,
        "norm_ffn": 1.0 + 0.02 * n(18, (DEPTH, D_MODEL)),
        "w_up": n(19, (DEPTH, D_MODEL, D_FF)) * D_MODEL ** -0.5,
        "w_down": n(20, (DEPTH, D_FF, D_MODEL)) * D_FF ** -0.5,
    }


def reference(x_prompt, x_sample, state_conv, state_pool, cache_k, cache_v,
              norm_mix, w_in, w_conv, b_conv, ln_conv_g, ln_conv_b, w_pool, pool_scale,
              q_norm, k_norm, attn_sinks, w_out, norm_ffn, w_up, w_down):
    Bp, Sp = x_prompt.shape[:2]
    Bs, Ts = x_sample.shape[:2]
    pos_p = jnp.arange(Sp, dtype=jnp.int32)
    pos_s = PAST_LEN + jnp.arange(Ts, dtype=jnp.int32)
    xp, xs = x_prompt, x_sample
    conv_p, pool_p, k_p, v_p = [], [], [], []
    conv_s, pool_s, k_s, v_s = [], [], [], []
    for l in range(DEPTH):
        p = {
            "norm_mix": norm_mix[l], "w_in": w_in[l], "w_conv": w_conv[l], "b_conv": b_conv[l],
            "ln_conv_g": ln_conv_g[l], "ln_conv_b": ln_conv_b[l], "w_pool": w_pool[l],
            "pool_scale": pool_scale[l], "q_norm": q_norm[l], "k_norm": k_norm[l],
            "attn_sinks": attn_sinks[l], "w_out": w_out[l], "norm_ffn": norm_ffn[l],
            "w_up": w_up[l], "w_down": w_down[l],
        }
        zc = jnp.zeros((Bp, CONV_BUF, D_CONV), xp.dtype)
        zp = jnp.zeros((Bp, POOL_BUF, D_POOL), xp.dtype)
        xp, c1, p1, k1, v1 = layer_forward(xp, pos_p, zc, zp, None, p)
        xs, c2, p2, k2, v2 = layer_forward(xs, pos_s, state_conv[l], state_pool[l],
                                           (cache_k[l], cache_v[l]), p)
        conv_p.append(c1); pool_p.append(p1); k_p.append(k1); v_p.append(v1)
        conv_s.append(c2); pool_s.append(p2); k_s.append(k2); v_s.append(v2)
    y_prompt, y_sample = xp, xs
    new_state_conv_prompt = jnp.stack(conv_p)
    new_state_pool_prompt = jnp.stack(pool_p)
    new_cache_k_prompt = jnp.stack(k_p)
    new_cache_v_prompt = jnp.stack(v_p)
    new_state_conv_sample = jnp.stack(conv_s)
    new_state_pool_sample = jnp.stack(pool_s)
    new_cache_k_sample = jnp.stack(k_s)
    new_cache_v_sample = jnp.stack(v_s)
    return (y_prompt, y_sample, new_state_conv_prompt, new_state_pool_prompt, new_cache_k_prompt,
            new_cache_v_prompt, new_state_conv_sample, new_state_pool_sample, new_cache_k_sample,
            new_cache_v_sample)
```

```python
import functools
import math

import jax
import jax.numpy as jnp
from jax import lax
from jax.experimental import pallas as pl
from jax.experimental.pallas import tpu as pltpu

F32 = jnp.float32
BF16 = jnp.bfloat16

HEAD_DIM = 128
N_KV_HEADS = 4
Q_PER_KV = 4
N_HEADS = N_KV_HEADS * Q_PER_KV
D_ATTN = N_HEADS * HEAD_DIM
D_KV = N_KV_HEADS * HEAD_DIM
D_CONV = 1024
D_POOL = 1024
ROT_DIM = HEAD_DIM // 4
ROPE_THETA = 500000.0
WINDOW = 128
ATTN_BLOCK = 128
KV_BUF = 128
CONV_WIDTH = 31
CONV_BUF = CONV_WIDTH - 1
POOL_WINDOWS = (2, 4, 8, 16)
POOL_GROUP = D_POOL // len(POOL_WINDOWS)
POOL_BUF = max(POOL_WINDOWS) - 1
PAST_LEN = 8192
EPS = 1e-6

OFF_A = 0
OFF_B = D_CONV
OFF_Z = 2 * D_CONV
OFF_Q = 2 * D_CONV + D_POOL
OFF_K = OFF_Q + D_ATTN
OFF_V = OFF_K + D_KV
D_IN = OFF_V + D_KV

V7X_VMEM_BYTES = 64 * 1024 * 1024
LANES = 128
SUBLANES_BF16 = 16

CONV_HALO = 32
POOL_HALO = 16


def _divisor_tile(n, target, multiple):
    best = None
    for d in range(multiple, min(n, target) + 1, multiple):
        if n % d == 0:
            best = d
    return n if best is None else best


def _vmem_limit(nbytes):
    return int(min(V7X_VMEM_BYTES - 4 * 1024 * 1024, nbytes + 8 * 1024 * 1024))


def _rmsnorm_kernel(x_ref, g_ref, o_ref):
    x = x_ref[...]
    ms = jnp.mean(x * x, axis=-1, keepdims=True)
    o_ref[...] = (x * lax.rsqrt(ms + EPS) * g_ref[...]).astype(o_ref.dtype)


def _rmsnorm(x, g):
    m, d = x.shape
    tr = _divisor_tile(m, 256, SUBLANES_BF16)
    return pl.pallas_call(
        _rmsnorm_kernel,
        grid=(m // tr,),
        in_specs=[pl.BlockSpec((tr, d), lambda i: (i, 0)),
                  pl.BlockSpec((1, d), lambda i: (0, 0))],
        out_specs=pl.BlockSpec((tr, d), lambda i: (i, 0)),
        out_shape=jax.ShapeDtypeStruct((m, d), BF16),
        compiler_params=pltpu.CompilerParams(dimension_semantics=("arbitrary",)),
        name="rmsnorm",
    )(x, g.reshape(1, d))


def _mm_kernel(*refs, nk, relu2, has_res):
    x_ref, w_ref = refs[0], refs[1]
    res_ref = refs[2] if has_res else None
    o_ref = refs[2 + int(has_res)]
    acc_ref = refs[3 + int(has_res)] if nk > 1 else None

    def finish(acc):
        if relu2:
            acc = jnp.square(jnp.maximum(acc, 0.0))
        if has_res:
            acc = res_ref[...] + acc
        o_ref[...] = acc.astype(o_ref.dtype)

    part = jnp.dot(x_ref[...], w_ref[...], preferred_element_type=F32)
    if nk == 1:
        finish(part)
    else:
        k = pl.program_id(2)

        @pl.when(k == 0)
        def _():
            acc_ref[...] = part

        @pl.when(k > 0)
        def _():
            acc_ref[...] += part

        @pl.when(k == nk - 1)
        def _():
            finish(acc_ref[...])


def _matmul(x, w, *, res=None, relu2=False, out_dtype=F32, tm_target=1024, tn_target=1024,
            tk_target=4096, name="matmul"):
    m, kdim = x.shape
    n = w.shape[1]
    tm = _divisor_tile(m, tm_target, SUBLANES_BF16)
    tn = _divisor_tile(n, tn_target, LANES)
    tk = _divisor_tile(kdim, tk_target, LANES)
    nk = kdim // tk
    has_res = res is not None
    in_specs = [pl.BlockSpec((tm, tk), lambda i, j, k: (i, k)),
                pl.BlockSpec((tk, tn), lambda i, j, k: (k, j))]
    args = [x, w]
    if has_res:
        in_specs.append(pl.BlockSpec((tm, tn), lambda i, j, k: (i, j)))
        args.append(res)
    out_bytes = jnp.dtype(out_dtype).itemsize
    vmem = 2 * (tm * tk * x.dtype.itemsize + tk * tn * w.dtype.itemsize + tm * tn * out_bytes)
    vmem += 2 * tm * tn * 4 if has_res else 0
    vmem += tm * tn * 4 if nk > 1 else 0
    vmem += 2 * tm * tn * 4
    return pl.pallas_call(
        functools.partial(_mm_kernel, nk=nk, relu2=relu2, has_res=has_res),
        grid=(m // tm, n // tn, nk),
        in_specs=in_specs,
        out_specs=pl.BlockSpec((tm, tn), lambda i, j, k: (i, j)),
        out_shape=jax.ShapeDtypeStruct((m, n), out_dtype),
        scratch_shapes=[pltpu.VMEM((tm, tn), F32)] if nk > 1 else [],
        compiler_params=pltpu.CompilerParams(
            dimension_semantics=("arbitrary", "arbitrary", "arbitrary"),
            vmem_limit_bytes=_vmem_limit(vmem)),
        name=name,
    )(*args)


def _sigmoid(x):
    return 1.0 / (1.0 + jnp.exp(-x))


def _layernorm_silu(c, g, b):
    mu = jnp.mean(c, axis=-1, keepdims=True)
    xc = c - mu
    var = jnp.mean(xc * xc, axis=-1, keepdims=True)
    y = xc * lax.rsqrt(var + EPS) * g + b
    return y * _sigmoid(y)


def _norm_rope(x, g, cos, sin_lo, sin_hi):
    y = x * lax.rsqrt(jnp.mean(x * x, axis=-1, keepdims=True) + EPS) * g
    half = ROT_DIM // 2
    up = pltpu.roll(y, HEAD_DIM - half, 1)
    down = pltpu.roll(y, half, 1)
    return y * cos + up * sin_lo + down * sin_hi


def _rope_tables(pos):
    half = ROT_DIM // 2
    inv_freq = jnp.exp(-math.log(ROPE_THETA) * 2.0 * jnp.arange(half, dtype=F32) / ROT_DIM)
    ang = pos.astype(F32)[:, None] * inv_freq[None, :]
    cos, sin = jnp.cos(ang), jnp.sin(ang)
    s = pos.shape[0]
    ones = jnp.ones((s, HEAD_DIM - ROT_DIM), F32)
    zeros = jnp.zeros((s, HEAD_DIM - ROT_DIM), F32)
    zh = jnp.zeros((s, half), F32)
    cos_t = jnp.concatenate([cos, cos, ones], axis=-1)
    sin_lo = jnp.concatenate([-sin, zh, zeros], axis=-1)
    sin_hi = jnp.concatenate([zh, sin, zeros], axis=-1)
    return cos_t, sin_lo, sin_hi


def _pool_group(zbuf, base, rows, g, w, pos0, wpool_ref, pscale_ref):
    ls = slice(g * POOL_GROUP, (g + 1) * POOL_GROUP)
    cur = zbuf[base:base + rows, ls]
    s = cur
    for i in range(1, w):
        s = s + zbuf[base - i:base - i + rows, ls]
    pos = pos0 + lax.broadcasted_iota(jnp.int32, (rows, 1), 0)
    cnt = jnp.minimum(pos + 1, w).astype(F32)
    p = (s / cnt - cur).astype(BF16)
    y = jnp.dot(p, wpool_ref[g].astype(BF16), preferred_element_type=F32)
    return y * pscale_ref[:, ls]


def _convpool_kernel(proj_ref, chist_ref, phist_ref, wconv_ref, bconv_ref, lng_ref, lnb_ref,
                     wpool_ref, pscale_ref, mix_ref, cnew_ref, pnew_ref, ubuf, zbuf, cbuf,
                     *, tile, n_tiles):
    t = pl.program_id(1)

    @pl.when(t == 0)
    def _():
        ubuf[0:CONV_HALO - CONV_BUF, :] = jnp.zeros((CONV_HALO - CONV_BUF, D_CONV), F32)
        ubuf[CONV_HALO - CONV_BUF:CONV_HALO, :] = chist_ref[0]
        zbuf[0:POOL_HALO - POOL_BUF, :] = jnp.zeros((POOL_HALO - POOL_BUF, D_POOL), F32)
        zbuf[POOL_HALO - POOL_BUF:POOL_HALO, :] = phist_ref[0]

    a = proj_ref[:, OFF_A:OFF_A + D_CONV]
    b = proj_ref[:, OFF_B:OFF_B + D_CONV]
    ubuf[CONV_HALO:CONV_HALO + tile, :] = a * _sigmoid(b)
    zbuf[POOL_HALO:POOL_HALO + tile, :] = proj_ref[:, OFF_Z:OFF_Z + D_POOL]

    rc, lc = 32, 256
    first = CONV_HALO - CONV_BUF

    for r0 in range(0, tile, rc):
        for c in range(D_CONV // lc):
            ls = slice(c * lc, (c + 1) * lc)
            acc = ubuf[r0 + first:r0 + first + rc, ls] * wconv_ref[0:1, ls]
            for k in range(1, CONV_WIDTH):
                acc = acc + ubuf[r0 + first + k:r0 + first + k + rc, ls] * wconv_ref[k:k + 1, ls]
            cbuf[r0:r0 + rc, ls] = acc + bconv_ref[:, ls]
    c = _layernorm_silu(cbuf[...], lng_ref[...], lnb_ref[...])
    mix_ref[:, 0:D_CONV] = c.astype(mix_ref.dtype)

    for g, w in enumerate(POOL_WINDOWS):
        y = _pool_group(zbuf, POOL_HALO, tile, g, w, t * tile, wpool_ref, pscale_ref)
        mix_ref[:, D_CONV + g * POOL_GROUP:D_CONV + (g + 1) * POOL_GROUP] = y.astype(mix_ref.dtype)

    @pl.when(t == n_tiles - 1)
    def _():
        cnew_ref[0] = ubuf[CONV_HALO + tile - CONV_BUF:CONV_HALO + tile, :]
        pnew_ref[0] = zbuf[POOL_HALO + tile - POOL_BUF:POOL_HALO + tile, :]

    ubuf[0:CONV_HALO, :] = ubuf[tile:tile + CONV_HALO, :]
    zbuf[0:POOL_HALO, :] = zbuf[tile:tile + POOL_HALO, :]


def _convpool_prompt(proj, chist, phist, wconv, bconv, lng, lnb, wpool, pscale, *, batch, seq):
    tile = _divisor_tile(seq, 256, CONV_HALO)
    n_tiles = seq // tile
    m = batch * seq
    row = lambda b, t: (b * n_tiles + t, 0)
    const2 = lambda b, t: (0, 0)
    per_batch = lambda b, t: (b, 0, 0)
    width = OFF_Q
    return pl.pallas_call(
        functools.partial(_convpool_kernel, tile=tile, n_tiles=n_tiles),
        grid=(batch, n_tiles),
        in_specs=[pl.BlockSpec((tile, width), row),
                  pl.BlockSpec((1, CONV_BUF, D_CONV), per_batch),
                  pl.BlockSpec((1, POOL_BUF, D_POOL), per_batch),
                  pl.BlockSpec((CONV_WIDTH, D_CONV), const2),
                  pl.BlockSpec((1, D_CONV), const2),
                  pl.BlockSpec((1, D_CONV), const2),
                  pl.BlockSpec((1, D_CONV), const2),
                  pl.BlockSpec((len(POOL_WINDOWS), POOL_GROUP, POOL_GROUP), lambda b, t: (0, 0, 0)),
                  pl.BlockSpec((1, D_POOL), const2)],
        out_specs=[pl.BlockSpec((tile, D_CONV + D_POOL), row),
                   pl.BlockSpec((1, CONV_BUF, D_CONV), per_batch),
                   pl.BlockSpec((1, POOL_BUF, D_POOL), per_batch)],
        out_shape=[jax.ShapeDtypeStruct((m, D_CONV + D_POOL + D_ATTN), BF16),
                   jax.ShapeDtypeStruct((batch, CONV_BUF, D_CONV), F32),
                   jax.ShapeDtypeStruct((batch, POOL_BUF, D_POOL), F32)],
        scratch_shapes=[pltpu.VMEM((CONV_HALO + tile, D_CONV), F32),
                        pltpu.VMEM((POOL_HALO + tile, D_POOL), F32),
                        pltpu.VMEM((tile, D_CONV), F32)],
        compiler_params=pltpu.CompilerParams(dimension_semantics=("arbitrary", "arbitrary")),
        name="convpool_prompt",
    )(proj, chist, phist, wconv, bconv.reshape(1, -1), lng.reshape(1, -1), lnb.reshape(1, -1),
      wpool, pscale.reshape(1, -1))


def _softmax_sink_pv(s, valid, sink_col, vv):
    s = jnp.where(valid, s, -jnp.inf)
    m = jnp.maximum(jnp.max(s, axis=-1, keepdims=True), sink_col)
    e = jnp.exp(s - m)
    den = jnp.sum(e, axis=-1, keepdims=True) + jnp.exp(sink_col - m)
    return jnp.dot(e.astype(BF16), vv, preferred_element_type=F32) / den


def _attn_prompt_kernel(sink_ref, q_ref, k_ref, v_ref, cos_ref, slo_ref, shi_ref, qn_ref, kn_ref,
                        mix_in_ref, o_ref, knew_ref, vnew_ref, qbuf, kbuf, vbuf, *, seq):
    del mix_in_ref
    kh = pl.program_id(1)
    blk = ATTN_BLOCK
    n_blocks = seq // blk
    prep_rows = _divisor_tile(seq, 256, blk)

    kbuf[0:blk, :] = jnp.zeros((blk, HEAD_DIM), BF16)
    vbuf[0:blk, :] = jnp.zeros((blk, HEAD_DIM), BF16)

    def prep(i, carry):
        r0 = pl.multiple_of(i * prep_rows, prep_rows)
        rows = pl.ds(r0, prep_rows)
        cos, slo, shi = cos_ref[rows, :], slo_ref[rows, :], shi_ref[rows, :]
        kbuf[pl.ds(blk + r0, prep_rows), :] = _norm_rope(k_ref[rows, :], kn_ref[...], cos, slo, shi).astype(BF16)
        vbuf[pl.ds(blk + r0, prep_rows), :] = v_ref[rows, :].astype(BF16)
        for g in range(Q_PER_KV):
            hs = slice(g * HEAD_DIM, (g + 1) * HEAD_DIM)
            qbuf[rows, hs] = _norm_rope(q_ref[rows, hs], qn_ref[...], cos, slo, shi).astype(BF16)
        return carry

    lax.fori_loop(0, seq // prep_rows, prep, 0)

    last = slice(seq - KV_BUF, seq)
    knew_ref[0] = _norm_rope(k_ref[last, :], kn_ref[...], cos_ref[last, :], slo_ref[last, :], shi_ref[last, :])
    vnew_ref[0] = v_ref[last, :]

    rows_all = Q_PER_KV * blk
    row = lax.broadcasted_iota(jnp.int32, (rows_all, 1), 0)
    qi = jnp.bitwise_and(row, blk - 1)
    kj = lax.broadcasted_iota(jnp.int32, (1, 2 * blk), 1)
    sink_col = jnp.zeros((rows_all, 1), F32)
    for g in range(Q_PER_KV):
        sink_col = jnp.where((row >= g * blk) & (row < (g + 1) * blk), sink_ref[kh * Q_PER_KV + g], sink_col)
    scale = HEAD_DIM ** -0.5

    def block(n, carry):
        r0 = pl.multiple_of(n * blk, blk)
        q4 = jnp.concatenate([qbuf[pl.ds(r0, blk), g * HEAD_DIM:(g + 1) * HEAD_DIM] for g in range(Q_PER_KV)], axis=0)
        kk = kbuf[pl.ds(r0, 2 * blk), :]
        vv = vbuf[pl.ds(r0, 2 * blk), :]
        s = lax.dot_general(q4, kk, (((1,), (1,)), ((), ())), preferred_element_type=F32) * scale
        lo = jnp.where(n == 0, blk, qi)
        valid = (kj >= lo) & (kj <= qi + WINDOW)
        o = _softmax_sink_pv(s, valid, sink_col, vv)
        for g in range(Q_PER_KV):
            o_ref[pl.ds(r0, blk), g * HEAD_DIM:(g + 1) * HEAD_DIM] = o[g * blk:(g + 1) * blk, :].astype(o_ref.dtype)
        return carry

    lax.fori_loop(0, n_blocks, block, 0)


def _attn_prompt(proj, mix, sinks, qn, kn, tables, *, batch, seq):
    m = batch * seq
    gw = Q_PER_KV * HEAD_DIM
    cos_t, sin_lo, sin_hi = tables
    const2 = lambda b, h: (0, 0)
    return pl.pallas_call(
        functools.partial(_attn_prompt_kernel, seq=seq),
        grid=(batch, N_KV_HEADS),
        in_specs=[pl.BlockSpec(memory_space=pltpu.SMEM),
                  pl.BlockSpec((seq, gw), lambda b, h: (b, OFF_Q // gw + h)),
                  pl.BlockSpec((seq, HEAD_DIM), lambda b, h: (b, OFF_K // HEAD_DIM + h)),
                  pl.BlockSpec((seq, HEAD_DIM), lambda b, h: (b, OFF_V // HEAD_DIM + h)),
                  pl.BlockSpec((seq, HEAD_DIM), const2),
                  pl.BlockSpec((seq, HEAD_DIM), const2),
                  pl.BlockSpec((seq, HEAD_DIM), const2),
                  pl.BlockSpec((1, HEAD_DIM), const2),
                  pl.BlockSpec((1, HEAD_DIM), const2),
                  pl.BlockSpec(memory_space=pl.ANY)],
        out_specs=[pl.BlockSpec((seq, gw), lambda b, h: (b, (D_CONV + D_POOL) // gw + h)),
                   pl.BlockSpec((1, KV_BUF, HEAD_DIM), lambda b, h: (b, 0, h)),
                   pl.BlockSpec((1, KV_BUF, HEAD_DIM), lambda b, h: (b, 0, h))],
        out_shape=[jax.ShapeDtypeStruct((m, D_CONV + D_POOL + D_ATTN), BF16),
                   jax.ShapeDtypeStruct((batch, KV_BUF, D_KV), F32),
                   jax.ShapeDtypeStruct((batch, KV_BUF, D_KV), F32)],
        scratch_shapes=[pltpu.VMEM((seq, gw), BF16),
                        pltpu.VMEM((seq + ATTN_BLOCK, HEAD_DIM), BF16),
                        pltpu.VMEM((seq + ATTN_BLOCK, HEAD_DIM), BF16)],
        input_output_aliases={9: 0},
        compiler_params=pltpu.CompilerParams(dimension_semantics=("arbitrary", "arbitrary")),
        name="attn_prompt",
    )(sinks, proj, proj, proj, cos_t, sin_lo, sin_hi, qn.reshape(1, -1), kn.reshape(1, -1), mix)


KEYS_PAD = 256


def _sample_kernel(sink_ref, proj_ref, chist_ref, phist_ref, ck_ref, cv_ref, wconv_ref, bconv_ref,
                   lng_ref, lnb_ref, wpool_ref, pscale_ref, qn_ref, kn_ref, cos_ref, slo_ref, shi_ref,
                   mix_ref, cnew_ref, pnew_ref, knew_ref, vnew_ref, fbuf, pbuf, kkbuf, vvbuf, qbuf, obuf,
                   *, steps):
    ts = steps
    a = proj_ref[0, :, OFF_A:OFF_A + D_CONV]
    b = proj_ref[0, :, OFF_B:OFF_B + D_CONV]
    fbuf[0:CONV_BUF, :] = chist_ref[0]
    fbuf[CONV_BUF:CONV_BUF + ts, :] = a * _sigmoid(b)
    acc = fbuf[0:ts, :] * wconv_ref[0:1, :]
    for k in range(1, CONV_WIDTH):
        acc = acc + fbuf[k:k + ts, :] * wconv_ref[k:k + 1, :]
    c = _layernorm_silu(acc + bconv_ref[...], lng_ref[...], lnb_ref[...])
    mix_ref[0, :, 0:D_CONV] = c
    cnew_ref[0] = fbuf[ts:ts + CONV_BUF, :]

    pbuf[0:POOL_BUF, :] = phist_ref[0]
    pbuf[POOL_BUF:POOL_BUF + ts, :] = proj_ref[0, :, OFF_Z:OFF_Z + D_POOL]
    for g, w in enumerate(POOL_WINDOWS):
        y = _pool_group(pbuf, POOL_BUF, ts, g, w, PAST_LEN, wpool_ref, pscale_ref)
        mix_ref[0, :, D_CONV + g * POOL_GROUP:D_CONV + (g + 1) * POOL_GROUP] = y
    pnew_ref[0] = pbuf[ts:ts + POOL_BUF, :]

    cos, slo, shi = cos_ref[...], slo_ref[...], shi_ref[...]
    kkbuf[0:KV_BUF, :] = ck_ref[0]
    vvbuf[0:KV_BUF, :] = cv_ref[0]
    for h in range(N_KV_HEADS):
        hs = slice(h * HEAD_DIM, (h + 1) * HEAD_DIM)
        kkbuf[KV_BUF:KV_BUF + ts, hs] = _norm_rope(
            proj_ref[0, :, OFF_K + h * HEAD_DIM:OFF_K + (h + 1) * HEAD_DIM], kn_ref[...], cos, slo, shi)
    vvbuf[KV_BUF:KV_BUF + ts, :] = proj_ref[0, :, OFF_V:OFF_V + D_KV]
    pad = KEYS_PAD - KV_BUF - ts
    kkbuf[KV_BUF + ts:KEYS_PAD, :] = jnp.zeros((pad, D_KV), F32)
    vvbuf[KV_BUF + ts:KEYS_PAD, :] = jnp.zeros((pad, D_KV), F32)
    knew_ref[0] = kkbuf[ts:ts + KV_BUF, :]
    vnew_ref[0] = vvbuf[ts:ts + KV_BUF, :]

    rows_all = Q_PER_KV * ts
    row = lax.broadcasted_iota(jnp.int32, (rows_all, 1), 0)
    tq = row % ts
    kj = lax.broadcasted_iota(jnp.int32, (1, KEYS_PAD), 1)
    valid = (kj >= tq + (KV_BUF - WINDOW)) & (kj <= tq + KV_BUF) & (kj < KV_BUF + ts)
    scale = HEAD_DIM ** -0.5
    for kh in range(N_KV_HEADS):
        hs = slice(kh * HEAD_DIM, (kh + 1) * HEAD_DIM)
        sink_col = jnp.zeros((rows_all, 1), F32)
        for g in range(Q_PER_KV):
            h = kh * Q_PER_KV + g
            qbuf[g * ts:(g + 1) * ts, :] = _norm_rope(
                proj_ref[0, :, OFF_Q + h * HEAD_DIM:OFF_Q + (h + 1) * HEAD_DIM], qn_ref[...], cos, slo, shi)
            sink_col = jnp.where((row >= g * ts) & (row < (g + 1) * ts), sink_ref[h], sink_col)
        q = qbuf[...].astype(BF16)
        kk = kkbuf[:, hs].astype(BF16)
        vv = vvbuf[:, hs].astype(BF16)
        s = lax.dot_general(q, kk, (((1,), (1,)), ((), ())), preferred_element_type=F32) * scale
        obuf[...] = _softmax_sink_pv(s, valid, sink_col, vv)
        for g in range(Q_PER_KV):
            h = kh * Q_PER_KV + g
            col = D_CONV + D_POOL + h * HEAD_DIM
            mix_ref[0, :, col:col + HEAD_DIM] = obuf[g * ts:(g + 1) * ts, :]


def _sample_mixers(proj, chist, phist, ck, cv, sinks, wconv, bconv, lng, lnb, wpool, pscale, qn, kn, tables):
    nb, ts, _ = proj.shape
    cos_t, sin_lo, sin_hi = tables
    per_batch = lambda b: (b, 0, 0)
    const2 = lambda b: (0, 0)
    d_mix = D_CONV + D_POOL + D_ATTN
    vec = lambda n: pl.BlockSpec((1, n), const2)
    return pl.pallas_call(
        functools.partial(_sample_kernel, steps=ts),
        grid=(nb,),
        in_specs=[pl.BlockSpec(memory_space=pltpu.SMEM),
                  pl.BlockSpec((1, ts, D_IN), per_batch),
                  pl.BlockSpec((1, CONV_BUF, D_CONV), per_batch),
                  pl.BlockSpec((1, POOL_BUF, D_POOL), per_batch),
                  pl.BlockSpec((1, KV_BUF, D_KV), per_batch),
                  pl.BlockSpec((1, KV_BUF, D_KV), per_batch),
                  pl.BlockSpec((CONV_WIDTH, D_CONV), const2),
                  vec(D_CONV), vec(D_CONV), vec(D_CONV),
                  pl.BlockSpec((len(POOL_WINDOWS), POOL_GROUP, POOL_GROUP), lambda b: (0, 0, 0)),
                  vec(D_POOL), vec(HEAD_DIM), vec(HEAD_DIM),
                  pl.BlockSpec((ts, HEAD_DIM), const2),
                  pl.BlockSpec((ts, HEAD_DIM), const2),
                  pl.BlockSpec((ts, HEAD_DIM), const2)],
        out_specs=[pl.BlockSpec((1, ts, d_mix), per_batch),
                   pl.BlockSpec((1, CONV_BUF, D_CONV), per_batch),
                   pl.BlockSpec((1, POOL_BUF, D_POOL), per_batch),
                   pl.BlockSpec((1, KV_BUF, D_KV), per_batch),
                   pl.BlockSpec((1, KV_BUF, D_KV), per_batch)],
        out_shape=[jax.ShapeDtypeStruct((nb, ts, d_mix), F32),
                   jax.ShapeDtypeStruct((nb, CONV_BUF, D_CONV), F32),
                   jax.ShapeDtypeStruct((nb, POOL_BUF, D_POOL), F32),
                   jax.ShapeDtypeStruct((nb, KV_BUF, D_KV), F32),
                   jax.ShapeDtypeStruct((nb, KV_BUF, D_KV), F32)],
        scratch_shapes=[pltpu.VMEM((CONV_BUF + 2 * ts + 2, D_CONV), F32),
                        pltpu.VMEM((POOL_BUF + 2 * ts + 1, D_POOL), F32),
                        pltpu.VMEM((KEYS_PAD, D_KV), F32),
                        pltpu.VMEM((KEYS_PAD, D_KV), F32),
                        pltpu.VMEM((Q_PER_KV * ts, HEAD_DIM), F32),
                        pltpu.VMEM((Q_PER_KV * ts, HEAD_DIM), F32)],
        compiler_params=pltpu.CompilerParams(dimension_semantics=("arbitrary",)),
        name="sample_mixers",
    )(sinks, proj, chist, phist, ck, cv, wconv, bconv.reshape(1, -1), lng.reshape(1, -1), lnb.reshape(1, -1),
      wpool, pscale.reshape(1, -1), qn.reshape(1, -1), kn.reshape(1, -1), cos_t, sin_lo, sin_hi)


def kernel(x_prompt, x_sample, state_conv, state_pool, cache_k, cache_v, norm_mix, w_in, w_conv, b_conv,
           ln_conv_g, ln_conv_b, w_pool, pool_scale, q_norm, k_norm, attn_sinks, w_out, norm_ffn, w_up, w_down):
    bp, sp, d_model = x_prompt.shape
    bs, ts, _ = x_sample.shape
    depth = w_in.shape[0]
    assert w_in.shape[2] == D_IN and sp % ATTN_BLOCK == 0 and KV_BUF == WINDOW

    xp = x_prompt.reshape(bp * sp, d_model)
    xs = x_sample.reshape(bs * ts, d_model)
    tab_p = _rope_tables(jnp.arange(sp, dtype=jnp.int32))
    tab_s = _rope_tables(PAST_LEN + jnp.arange(ts, dtype=jnp.int32))
    zero_conv = jnp.zeros((bp, CONV_BUF, D_CONV), F32)
    zero_pool = jnp.zeros((bp, POOL_BUF, D_POOL), F32)
    ck_all = cache_k.reshape(depth, bs, KV_BUF, D_KV)
    cv_all = cache_v.reshape(depth, bs, KV_BUF, D_KV)

    outs = [[] for _ in range(8)]
    for l in range(depth):
        w_in_l = w_in[l].astype(BF16)
        w_out_l = w_out[l].astype(BF16)
        w_up_l = w_up[l].astype(BF16)
        w_down_l = w_down[l].astype(BF16)
        mixer_w = (w_conv[l], b_conv[l], ln_conv_g[l], ln_conv_b[l], w_pool[l], pool_scale[l])

        proj_p = _matmul(_rmsnorm(xp, norm_mix[l]), w_in_l, name="in_proj")
        mix_p, conv_p, pool_p = _convpool_prompt(proj_p, zero_conv, zero_pool, *mixer_w, batch=bp, seq=sp)
        mix_p, k_p, v_p = _attn_prompt(proj_p, mix_p, attn_sinks[l], q_norm[l], k_norm[l], tab_p,
                                       batch=bp, seq=sp)
        xp = _matmul(mix_p, w_out_l, res=xp, tn_target=512, name="out_proj")

        proj_s = _matmul(_rmsnorm(xs, norm_mix[l]), w_in_l, name="in_proj_s").reshape(bs, ts, D_IN)
        mix_s, conv_s, pool_s, k_s, v_s = _sample_mixers(
            proj_s, state_conv[l], state_pool[l], ck_all[l], cv_all[l], attn_sinks[l], *mixer_w,
            q_norm[l], k_norm[l], tab_s)
        xs = _matmul(mix_s.reshape(bs * ts, -1).astype(BF16), w_out_l, res=xs, name="out_proj_s")

        up_p = _matmul(_rmsnorm(xp, norm_ffn[l]), w_up_l, relu2=True, out_dtype=BF16, name="ffn_up")
        xp = _matmul(up_p, w_down_l, res=xp, tk_target=2048, name="ffn_down")
        up_s = _matmul(_rmsnorm(xs, norm_ffn[l]), w_up_l, relu2=True, out_dtype=BF16, name="ffn_up_s")
        xs = _matmul(up_s, w_down_l, res=xs, tk_target=2048, name="ffn_down_s")

        for lst, val in zip(outs, (conv_p, pool_p, k_p, v_p, conv_s, pool_s, k_s, v_s)):
            lst.append(val)

    conv_p, pool_p, k_p, v_p, conv_s, pool_s, k_s, v_s = [jnp.stack(o) for o in outs]
    kv_shape_p = (depth, bp, KV_BUF, N_KV_HEADS, HEAD_DIM)
    kv_shape_s = (depth, bs, KV_BUF, N_KV_HEADS, HEAD_DIM)
    return (xp.reshape(bp, sp, d_model), xs.reshape(bs, ts, d_model),
            conv_p, pool_p, k_p.reshape(kv_shape_p), v_p.reshape(kv_shape_p),
            conv_s, pool_s, k_s.reshape(kv_shape_s), v_s.reshape(kv_shape_s))
```

```python
import functools
import math

import jax
import jax.numpy as jnp
from jax import lax
from jax.experimental import pallas as pl
from jax.experimental.pallas import tpu as pltpu

F32 = jnp.float32
BF16 = jnp.bfloat16

HEAD_DIM = 128
N_KV_HEADS = 4
Q_PER_KV = 4
N_HEADS = N_KV_HEADS * Q_PER_KV
D_ATTN = N_HEADS * HEAD_DIM
D_KV = N_KV_HEADS * HEAD_DIM
D_CONV = 1024
D_POOL = 1024
D_MIX = D_CONV + D_POOL + D_ATTN
ROT_DIM = HEAD_DIM // 4
ROPE_THETA = 500000.0
WINDOW = 128
ATTN_BLOCK = 128
KV_BUF = 128
CONV_WIDTH = 31
CONV_BUF = CONV_WIDTH - 1
POOL_WINDOWS = (2, 4, 8, 16)
POOL_GROUP = D_POOL // len(POOL_WINDOWS)
POOL_BUF = max(POOL_WINDOWS) - 1
PAST_LEN = 8192
EPS = 1e-6

OFF_A = 0
OFF_B = D_CONV
OFF_Z = 2 * D_CONV
OFF_Q = 2 * D_CONV + D_POOL
OFF_K = OFF_Q + D_ATTN
OFF_V = OFF_K + D_KV
D_IN = OFF_V + D_KV

V7X_VMEM_BYTES = 64 * 1024 * 1024
LANES = 128
SUBLANES = 8
SUBLANES_BF16 = 16

CONV_HALO = 32
POOL_HALO = 16


def _divisor_tile(n, target, multiple):
    best = None
    for d in range(multiple, min(n, target) + 1, multiple):
        if n % d == 0:
            best = d
    return n if best is None else best


def _vmem_limit(nbytes):
    return int(min(V7X_VMEM_BYTES - 2 * 1024 * 1024, nbytes + 6 * 1024 * 1024))


def _rmsnorm_kernel(x_ref, g_ref, o_ref):
    x = x_ref[...]
    ms = jnp.mean(x * x, axis=-1, keepdims=True)
    o_ref[...] = (x * lax.rsqrt(ms + EPS) * g_ref[...]).astype(o_ref.dtype)


def _rmsnorm(x, g):
    m, d = x.shape
    tr = _divisor_tile(m, 384, SUBLANES_BF16)
    return pl.pallas_call(
        _rmsnorm_kernel,
        grid=(m // tr,),
        in_specs=[pl.BlockSpec((tr, d), lambda i: (i, 0)),
                  pl.BlockSpec((1, d), lambda i: (0, 0))],
        out_specs=pl.BlockSpec((tr, d), lambda i: (i, 0)),
        out_shape=jax.ShapeDtypeStruct((m, d), BF16),
        compiler_params=pltpu.CompilerParams(dimension_semantics=("arbitrary",)),
        name="rmsnorm",
    )(x, g.reshape(1, d))


def _wsmm_kernel(*refs, nk, ni, tm, ck, relu2, has_res):
    x_ref, w_ref = refs[0], refs[1]
    res_ref = refs[2] if has_res else None
    o_ref = refs[2 + int(has_res)]
    wbf = refs[3 + int(has_res)]
    acc_ref = refs[4 + int(has_res)] if nk > 1 else None
    p = pl.program_id(0)
    i = pl.program_id(1)

    def cast_chunk():
        rows = pl.ds(pl.multiple_of(i * ck, ck), ck)
        wbf[p % 2, rows, :] = w_ref[...].astype(BF16)

    def finish(acc):
        if relu2:
            acc = jnp.square(jnp.maximum(acc, 0.0))
        if has_res:
            acc = res_ref[...] + acc
        o_ref[...] = acc.astype(o_ref.dtype)

    rows = pl.ds(pl.multiple_of(i * tm, tm), tm)

    @pl.when(p == 0)
    def _():
        cast_chunk()
        if nk > 1:
            acc_ref[rows, :] = jnp.zeros((tm, acc_ref.shape[1]), F32)

    @pl.when(p > 0)
    def _():
        part = jnp.dot(x_ref[...], wbf[(p + 1) % 2], preferred_element_type=F32)
        cast_chunk()
        if nk == 1:
            finish(part)
        else:
            k = (p - 1) % nk
            total = acc_ref[rows, :] + part
            finish(total)
            acc_ref[rows, :] = jnp.where(k == nk - 1, 0.0, total)


def _wsmm(x, w, layer, *, res=None, relu2=False, out_dtype=F32, tm_target=1040, tn=1024, tk=4096, name="wsmm"):
    m, kdim = x.shape
    n = w.shape[2]
    assert w.shape[1] == kdim and kdim % tk == 0 and n % tn == 0
    tm = _divisor_tile(m, tm_target, SUBLANES_BF16)
    ni, nk, nj = m // tm, kdim // tk, n // tn
    nt = nj * nk
    assert tk % ni == 0 and (tk // ni) % SUBLANES_BF16 == 0
    ck = tk // ni
    has_res = res is not None

    def tile_of(p):
        t = jnp.maximum(p - 1, 0)
        return t // nk, t % nk

    def x_map(p, i):
        _, k = tile_of(p)
        return jnp.where(p == 0, 0, i), k

    def w_map(p, i):
        t = jnp.minimum(p, nt - 1)
        return layer, (t % nk) * ni + jnp.where(p < nt, i, ni - 1), t // nk

    def o_map(p, i):
        j, k = tile_of(p)
        return jnp.where((p > 0) & (k == nk - 1), i, 0), j

    in_specs = [pl.BlockSpec((tm, tk), x_map), pl.BlockSpec((None, ck, tn), w_map)]
    args = [x, w]
    if has_res:
        in_specs.append(pl.BlockSpec((tm, tn), o_map))
        args.append(res)
    scratch = [pltpu.VMEM((2, tk, tn), BF16)]
    if nk > 1:
        scratch.append(pltpu.VMEM((m, tn), F32))
    out_bytes = jnp.dtype(out_dtype).itemsize
    vmem = 2 * (tm * tk * 2 + ck * tn * 4 + tm * tn * out_bytes) + 2 * tk * tn * 2
    vmem += 2 * tm * tn * 4 if has_res else 0
    vmem += m * tn * 4 if nk > 1 else 0
    vmem += tm * tn * 4
    return pl.pallas_call(
        functools.partial(_wsmm_kernel, nk=nk, ni=ni, tm=tm, ck=ck, relu2=relu2, has_res=has_res),
        grid=(nt + 1, ni),
        in_specs=in_specs,
        out_specs=pl.BlockSpec((tm, tn), o_map),
        out_shape=jax.ShapeDtypeStruct((m, n), out_dtype),
        scratch_shapes=scratch,
        compiler_params=pltpu.CompilerParams(
            dimension_semantics=("arbitrary", "arbitrary"),
            vmem_limit_bytes=_vmem_limit(vmem)),
        name=name,
    )(*args)


def _sigmoid(x):
    return 1.0 / (1.0 + jnp.exp(-x))


def _layernorm_silu(c, g, b):
    mu = jnp.mean(c, axis=-1, keepdims=True)
    xc = c - mu
    var = jnp.mean(xc * xc, axis=-1, keepdims=True)
    y = xc * lax.rsqrt(var + EPS) * g + b
    return y * _sigmoid(y)


def _norm_rope(x, g, cos, sin_lo, sin_hi):
    y = x * lax.rsqrt(jnp.mean(x * x, axis=-1, keepdims=True) + EPS) * g
    half = ROT_DIM // 2
    up = pltpu.roll(y, HEAD_DIM - half, 1)
    down = pltpu.roll(y, half, 1)
    return y * cos + up * sin_lo + down * sin_hi


def _rope_tables(pos):
    half = ROT_DIM // 2
    inv_freq = jnp.exp(-math.log(ROPE_THETA) * 2.0 * jnp.arange(half, dtype=F32) / ROT_DIM)
    ang = pos.astype(F32)[:, None] * inv_freq[None, :]
    cos, sin = jnp.cos(ang), jnp.sin(ang)
    s = pos.shape[0]
    ones = jnp.ones((s, HEAD_DIM - ROT_DIM), F32)
    zeros = jnp.zeros((s, HEAD_DIM - ROT_DIM), F32)
    zh = jnp.zeros((s, half), F32)
    cos_t = jnp.concatenate([cos, cos, ones], axis=-1)
    sin_lo = jnp.concatenate([-sin, zh, zeros], axis=-1)
    sin_hi = jnp.concatenate([zh, sin, zeros], axis=-1)
    return cos_t, sin_lo, sin_hi


def _pool_group(zbuf, base, rows, g, w, pos0, wpool_ref, pscale_ref):
    ls = slice(g * POOL_GROUP, (g + 1) * POOL_GROUP)
    cur = zbuf[base:base + rows, ls]
    s = cur
    for i in range(1, w):
        s = s + zbuf[base - i:base - i + rows, ls]
    pos = pos0 + lax.broadcasted_iota(jnp.int32, (rows, 1), 0)
    cnt = jnp.minimum(pos + 1, w).astype(F32)
    p = (s / cnt - cur).astype(BF16)
    y = jnp.dot(p, wpool_ref[g].astype(BF16), preferred_element_type=F32)
    return y * pscale_ref[:, ls]


def _convpool_kernel(proj_ref, chist_ref, phist_ref, wconv_ref, bconv_ref, lng_ref, lnb_ref,
                     wpool_ref, pscale_ref, mix_ref, cnew_ref, pnew_ref, ubuf, ush, zbuf, cbuf,
                     *, tile, n_tiles):
    t = pl.program_id(1)

    @pl.when(t == 0)
    def _():
        ubuf[0:CONV_HALO - CONV_BUF, :] = jnp.zeros((CONV_HALO - CONV_BUF, D_CONV), F32)
        ubuf[CONV_HALO - CONV_BUF:CONV_HALO, :] = chist_ref[0]
        zbuf[0:POOL_HALO - POOL_BUF, :] = jnp.zeros((POOL_HALO - POOL_BUF, D_POOL), F32)
        zbuf[POOL_HALO - POOL_BUF:POOL_HALO, :] = phist_ref[0]

    a = proj_ref[:, OFF_A:OFF_A + D_CONV]
    b = proj_ref[:, OFF_B:OFF_B + D_CONV]
    ubuf[CONV_HALO:CONV_HALO + tile, :] = a * _sigmoid(b)
    zbuf[POOL_HALO:POOL_HALO + tile, :] = proj_ref[:, OFF_Z:OFF_Z + D_POOL]

    span = tile + CONV_HALO - SUBLANES
    for s in range(1, SUBLANES):
        ush[s - 1, 0:span, :] = ubuf[s:s + span, :]
    rc, lc = 64, 256
    first = CONV_HALO - CONV_BUF
    for r0 in range(0, tile, rc):
        for c in range(D_CONV // lc):
            ls = slice(c * lc, (c + 1) * lc)
            acc = None
            for k in range(CONV_WIDTH):
                q, s = divmod(first + k, SUBLANES)
                rows = slice(r0 + q * SUBLANES, r0 + q * SUBLANES + rc)
                src = ubuf[rows, ls] if s == 0 else ush[s - 1, rows, ls]
                term = src * wconv_ref[k:k + 1, ls]
                acc = term if acc is None else acc + term
            cbuf[r0:r0 + rc, ls] = acc + bconv_ref[:, ls]
    c = _layernorm_silu(cbuf[...], lng_ref[...], lnb_ref[...])
    mix_ref[:, 0:D_CONV] = c.astype(mix_ref.dtype)

    for g, w in enumerate(POOL_WINDOWS):
        y = _pool_group(zbuf, POOL_HALO, tile, g, w, t * tile, wpool_ref, pscale_ref)
        mix_ref[:, D_CONV + g * POOL_GROUP:D_CONV + (g + 1) * POOL_GROUP] = y.astype(mix_ref.dtype)

    @pl.when(t == n_tiles - 1)
    def _():
        cnew_ref[0] = ubuf[CONV_HALO + tile - CONV_BUF:CONV_HALO + tile, :]
        pnew_ref[0] = zbuf[POOL_HALO + tile - POOL_BUF:POOL_HALO + tile, :]

    ubuf[0:CONV_HALO, :] = ubuf[tile:tile + CONV_HALO, :]
    zbuf[0:POOL_HALO, :] = zbuf[tile:tile + POOL_HALO, :]


def _convpool_prompt(proj, chist, phist, wconv, bconv, lng, lnb, wpool, pscale, *, batch, seq):
    tile = _divisor_tile(seq, 256, 2 * CONV_HALO)
    n_tiles = seq // tile
    m = proj.shape[0]
    row = lambda b, t: (b * n_tiles + t, 0)
    const2 = lambda b, t: (0, 0)
    per_batch = lambda b, t: (b, 0, 0)
    width = OFF_Q
    return pl.pallas_call(
        functools.partial(_convpool_kernel, tile=tile, n_tiles=n_tiles),
        grid=(batch, n_tiles),
        in_specs=[pl.BlockSpec((tile, width), row),
                  pl.BlockSpec((1, CONV_BUF, D_CONV), per_batch),
                  pl.BlockSpec((1, POOL_BUF, D_POOL), per_batch),
                  pl.BlockSpec((CONV_WIDTH, D_CONV), const2),
                  pl.BlockSpec((1, D_CONV), const2),
                  pl.BlockSpec((1, D_CONV), const2),
                  pl.BlockSpec((1, D_CONV), const2),
                  pl.BlockSpec((len(POOL_WINDOWS), POOL_GROUP, POOL_GROUP), lambda b, t: (0, 0, 0)),
                  pl.BlockSpec((1, D_POOL), const2)],
        out_specs=[pl.BlockSpec((tile, D_CONV + D_POOL), row),
                   pl.BlockSpec((1, CONV_BUF, D_CONV), per_batch),
                   pl.BlockSpec((1, POOL_BUF, D_POOL), per_batch)],
        out_shape=[jax.ShapeDtypeStruct((m, D_MIX), BF16),
                   jax.ShapeDtypeStruct((batch, CONV_BUF, D_CONV), F32),
                   jax.ShapeDtypeStruct((batch, POOL_BUF, D_POOL), F32)],
        scratch_shapes=[pltpu.VMEM((CONV_HALO + tile, D_CONV), F32),
                        pltpu.VMEM((SUBLANES - 1, CONV_HALO + tile, D_CONV), F32),
                        pltpu.VMEM((POOL_HALO + tile, D_POOL), F32),
                        pltpu.VMEM((tile, D_CONV), F32)],
        compiler_params=pltpu.CompilerParams(dimension_semantics=("arbitrary", "arbitrary")),
        name="convpool_prompt",
    )(proj, chist, phist, wconv, bconv.reshape(1, -1), lng.reshape(1, -1), lnb.reshape(1, -1),
      wpool, pscale.reshape(1, -1))


PREP_ROWS = 512
BLOCK_UNROLL = 4


def _softmax_sink_pv(s, sink_col, vv):
    m = jnp.maximum(jnp.max(s, axis=-1, keepdims=True), sink_col)
    e = jnp.exp(s - m)
    den = jnp.sum(e, axis=-1, keepdims=True) + jnp.exp(sink_col - m)
    return jnp.dot(e.astype(BF16), vv, preferred_element_type=F32) / den


def _attn_prompt_kernel(sink_ref, q_ref, k_ref, v_ref, cos_ref, slo_ref, shi_ref, qn_ref, kn_ref,
                        mix_in_ref, o_ref, knew_ref, vnew_ref, qbuf, kbuf, vbuf, bias, *, seq):
    del mix_in_ref
    kh = pl.program_id(1)
    blk = ATTN_BLOCK
    n_blocks = seq // blk
    rows_all = Q_PER_KV * blk

    kbuf[0:blk, :] = jnp.zeros((blk, HEAD_DIM), BF16)
    vbuf[0:blk, :] = jnp.zeros((blk, HEAD_DIM), BF16)

    row = lax.broadcasted_iota(jnp.int32, (rows_all, 1), 0)
    qi = jnp.bitwise_and(row, blk - 1)
    kj = lax.broadcasted_iota(jnp.int32, (1, 2 * blk), 1)
    band = (kj >= qi) & (kj <= qi + WINDOW)
    bias[1] = jnp.where(band, 0.0, -jnp.inf)
    bias[0] = jnp.where(band & (kj >= blk), 0.0, -jnp.inf)
    sink_col = jnp.zeros((rows_all, 1), F32)
    for g in range(Q_PER_KV):
        sink_col = jnp.where((row >= g * blk) & (row < (g + 1) * blk), sink_ref[kh * Q_PER_KV + g], sink_col)
    scale = HEAD_DIM ** -0.5

    last = slice(seq - KV_BUF, seq)
    knew_ref[0] = _norm_rope(k_ref[last, :], kn_ref[...], cos_ref[last, :], slo_ref[last, :], shi_ref[last, :])
    vnew_ref[0] = v_ref[last, :]

    prep_rows = _divisor_tile(seq, PREP_ROWS, blk)

    def prep(n, carry):
        r0 = pl.multiple_of(n * prep_rows, prep_rows)
        rows = pl.ds(r0, prep_rows)
        cos, slo, shi = cos_ref[rows, :], slo_ref[rows, :], shi_ref[rows, :]
        kbuf[pl.ds(blk + r0, prep_rows), :] = _norm_rope(k_ref[rows, :], kn_ref[...], cos, slo, shi).astype(BF16)
        vbuf[pl.ds(blk + r0, prep_rows), :] = v_ref[rows, :].astype(BF16)
        for g in range(Q_PER_KV):
            hs = slice(g * HEAD_DIM, (g + 1) * HEAD_DIM)
            qbuf[rows, hs] = _norm_rope(q_ref[rows, hs], qn_ref[...], cos, slo, shi).astype(BF16)
        return carry

    lax.fori_loop(0, seq // prep_rows, prep, 0)

    def block(n, carry):
        r0 = pl.multiple_of(n * blk, blk)
        rows = pl.ds(r0, blk)
        q4 = jnp.concatenate([qbuf[rows, g * HEAD_DIM:(g + 1) * HEAD_DIM] for g in range(Q_PER_KV)], axis=0)
        kk = kbuf[pl.ds(r0, 2 * blk), :]
        vv = vbuf[pl.ds(r0, 2 * blk), :]
        s = lax.dot_general(q4, kk, (((1,), (1,)), ((), ())), preferred_element_type=F32) * scale
        s = s + bias[jnp.minimum(n, 1)]
        o = _softmax_sink_pv(s, sink_col, vv)
        for g in range(Q_PER_KV):
            o_ref[rows, g * HEAD_DIM:(g + 1) * HEAD_DIM] = o[g * blk:(g + 1) * blk, :].astype(o_ref.dtype)
        return carry

    lax.fori_loop(0, n_blocks, block, 0, unroll=BLOCK_UNROLL)


def _attn_prompt(proj, mix, sinks, qn, kn, tables, *, batch, seq):
    m = proj.shape[0]
    gw = Q_PER_KV * HEAD_DIM
    cos_t, sin_lo, sin_hi = tables
    const2 = lambda b, h: (0, 0)
    return pl.pallas_call(
        functools.partial(_attn_prompt_kernel, seq=seq),
        grid=(batch, N_KV_HEADS),
        in_specs=[pl.BlockSpec(memory_space=pltpu.SMEM),
                  pl.BlockSpec((seq, gw), lambda b, h: (b, OFF_Q // gw + h)),
                  pl.BlockSpec((seq, HEAD_DIM), lambda b, h: (b, OFF_K // HEAD_DIM + h)),
                  pl.BlockSpec((seq, HEAD_DIM), lambda b, h: (b, OFF_V // HEAD_DIM + h)),
                  pl.BlockSpec((seq, HEAD_DIM), const2),
                  pl.BlockSpec((seq, HEAD_DIM), const2),
                  pl.BlockSpec((seq, HEAD_DIM), const2),
                  pl.BlockSpec((1, HEAD_DIM), const2),
                  pl.BlockSpec((1, HEAD_DIM), const2),
                  pl.BlockSpec(memory_space=pl.ANY)],
        out_specs=[pl.BlockSpec((seq, gw), lambda b, h: (b, (D_CONV + D_POOL) // gw + h)),
                   pl.BlockSpec((1, KV_BUF, HEAD_DIM), lambda b, h: (b, 0, h)),
                   pl.BlockSpec((1, KV_BUF, HEAD_DIM), lambda b, h: (b, 0, h))],
        out_shape=[jax.ShapeDtypeStruct((m, D_MIX), BF16),
                   jax.ShapeDtypeStruct((batch, KV_BUF, D_KV), F32),
                   jax.ShapeDtypeStruct((batch, KV_BUF, D_KV), F32)],
        scratch_shapes=[pltpu.VMEM((seq, gw), BF16),
                        pltpu.VMEM((seq + ATTN_BLOCK, HEAD_DIM), BF16),
                        pltpu.VMEM((seq + ATTN_BLOCK, HEAD_DIM), BF16),
                        pltpu.VMEM((2, Q_PER_KV * ATTN_BLOCK, 2 * ATTN_BLOCK), F32)],
        input_output_aliases={9: 0},
        compiler_params=pltpu.CompilerParams(dimension_semantics=("arbitrary", "arbitrary")),
        name="attn_prompt",
    )(sinks, proj, proj, proj, cos_t, sin_lo, sin_hi, qn.reshape(1, -1), kn.reshape(1, -1), mix)


KEYS_PAD = 256


def _sample_kernel(sink_ref, proj_ref, chist_ref, phist_ref, ck_ref, cv_ref, wconv_ref, bconv_ref,
                   lng_ref, lnb_ref, wpool_ref, pscale_ref, qn_ref, kn_ref, cos_ref, slo_ref, shi_ref,
                   mix_ref, cnew_ref, pnew_ref, knew_ref, vnew_ref, fbuf, pbuf, kkbuf, vvbuf, qbuf, obuf,
                   *, steps):
    ts = steps
    a = proj_ref[0, :, OFF_A:OFF_A + D_CONV]
    b = proj_ref[0, :, OFF_B:OFF_B + D_CONV]
    fbuf[0:CONV_BUF, :] = chist_ref[0]
    fbuf[CONV_BUF:CONV_BUF + ts, :] = a * _sigmoid(b)
    acc = fbuf[0:ts, :] * wconv_ref[0:1, :]
    for k in range(1, CONV_WIDTH):
        acc = acc + fbuf[k:k + ts, :] * wconv_ref[k:k + 1, :]
    c = _layernorm_silu(acc + bconv_ref[...], lng_ref[...], lnb_ref[...])
    mix_ref[0, :, 0:D_CONV] = c
    cnew_ref[0] = fbuf[ts:ts + CONV_BUF, :]

    pbuf[0:POOL_BUF, :] = phist_ref[0]
    pbuf[POOL_BUF:POOL_BUF + ts, :] = proj_ref[0, :, OFF_Z:OFF_Z + D_POOL]
    for g, w in enumerate(POOL_WINDOWS):
        y = _pool_group(pbuf, POOL_BUF, ts, g, w, PAST_LEN, wpool_ref, pscale_ref)
        mix_ref[0, :, D_CONV + g * POOL_GROUP:D_CONV + (g + 1) * POOL_GROUP] = y
    pnew_ref[0] = pbuf[ts:ts + POOL_BUF, :]

    cos, slo, shi = cos_ref[...], slo_ref[...], shi_ref[...]
    kkbuf[0:KV_BUF, :] = ck_ref[0]
    vvbuf[0:KV_BUF, :] = cv_ref[0]
    for h in range(N_KV_HEADS):
        hs = slice(h * HEAD_DIM, (h + 1) * HEAD_DIM)
        kkbuf[KV_BUF:KV_BUF + ts, hs] = _norm_rope(
            proj_ref[0, :, OFF_K + h * HEAD_DIM:OFF_K + (h + 1) * HEAD_DIM], kn_ref[...], cos, slo, shi)
    vvbuf[KV_BUF:KV_BUF + ts, :] = proj_ref[0, :, OFF_V:OFF_V + D_KV]
    pad = KEYS_PAD - KV_BUF - ts
    kkbuf[KV_BUF + ts:KEYS_PAD, :] = jnp.zeros((pad, D_KV), F32)
    vvbuf[KV_BUF + ts:KEYS_PAD, :] = jnp.zeros((pad, D_KV), F32)
    knew_ref[0] = kkbuf[ts:ts + KV_BUF, :]
    vnew_ref[0] = vvbuf[ts:ts + KV_BUF, :]

    rows_all = Q_PER_KV * ts
    row = lax.broadcasted_iota(jnp.int32, (rows_all, 1), 0)
    tq = row % ts
    kj = lax.broadcasted_iota(jnp.int32, (1, KEYS_PAD), 1)
    valid = (kj >= tq + (KV_BUF - WINDOW)) & (kj <= tq + KV_BUF) & (kj < KV_BUF + ts)
    scale = HEAD_DIM ** -0.5
    for kh in range(N_KV_HEADS):
        hs = slice(kh * HEAD_DIM, (kh + 1) * HEAD_DIM)
        sink_col = jnp.zeros((rows_all, 1), F32)
        for g in range(Q_PER_KV):
            h = kh * Q_PER_KV + g
            qbuf[g * ts:(g + 1) * ts, :] = _norm_rope(
                proj_ref[0, :, OFF_Q + h * HEAD_DIM:OFF_Q + (h + 1) * HEAD_DIM], qn_ref[...], cos, slo, shi)
            sink_col = jnp.where((row >= g * ts) & (row < (g + 1) * ts), sink_ref[h], sink_col)
        q = qbuf[...].astype(BF16)
        kk = kkbuf[:, hs].astype(BF16)
        vv = vvbuf[:, hs].astype(BF16)
        s = lax.dot_general(q, kk, (((1,), (1,)), ((), ())), preferred_element_type=F32) * scale
        s = jnp.where(valid, s, -jnp.inf)
        obuf[...] = _softmax_sink_pv(s, sink_col, vv)
        for g in range(Q_PER_KV):
            h = kh * Q_PER_KV + g
            col = D_CONV + D_POOL + h * HEAD_DIM
            mix_ref[0, :, col:col + HEAD_DIM] = obuf[g * ts:(g + 1) * ts, :]


def _sample_mixers(proj, chist, phist, ck, cv, sinks, wconv, bconv, lng, lnb, wpool, pscale, qn, kn, tables):
    nb, ts, _ = proj.shape
    cos_t, sin_lo, sin_hi = tables
    per_batch = lambda b: (b, 0, 0)
    const2 = lambda b: (0, 0)
    vec = lambda n: pl.BlockSpec((1, n), const2)
    return pl.pallas_call(
        functools.partial(_sample_kernel, steps=ts),
        grid=(nb,),
        in_specs=[pl.BlockSpec(memory_space=pltpu.SMEM),
                  pl.BlockSpec((1, ts, D_IN), per_batch),
                  pl.BlockSpec((1, CONV_BUF, D_CONV), per_batch),
                  pl.BlockSpec((1, POOL_BUF, D_POOL), per_batch),
                  pl.BlockSpec((1, KV_BUF, D_KV), per_batch),
                  pl.BlockSpec((1, KV_BUF, D_KV), per_batch),
                  pl.BlockSpec((CONV_WIDTH, D_CONV), const2),
                  vec(D_CONV), vec(D_CONV), vec(D_CONV),
                  pl.BlockSpec((len(POOL_WINDOWS), POOL_GROUP, POOL_GROUP), lambda b: (0, 0, 0)),
                  vec(D_POOL), vec(HEAD_DIM), vec(HEAD_DIM),
                  pl.BlockSpec((ts, HEAD_DIM), const2),
                  pl.BlockSpec((ts, HEAD_DIM), const2),
                  pl.BlockSpec((ts, HEAD_DIM), const2)],
        out_specs=[pl.BlockSpec((1, ts, D_MIX), per_batch),
                   pl.BlockSpec((1, CONV_BUF, D_CONV), per_batch),
                   pl.BlockSpec((1, POOL_BUF, D_POOL), per_batch),
                   pl.BlockSpec((1, KV_BUF, D_KV), per_batch),
                   pl.BlockSpec((1, KV_BUF, D_KV), per_batch)],
        out_shape=[jax.ShapeDtypeStruct((nb, ts, D_MIX), F32),
                   jax.ShapeDtypeStruct((nb, CONV_BUF, D_CONV), F32),
                   jax.ShapeDtypeStruct((nb, POOL_BUF, D_POOL), F32),
                   jax.ShapeDtypeStruct((nb, KV_BUF, D_KV), F32),
                   jax.ShapeDtypeStruct((nb, KV_BUF, D_KV), F32)],
        scratch_shapes=[pltpu.VMEM((CONV_BUF + 2 * ts + 2, D_CONV), F32),
                        pltpu.VMEM((POOL_BUF + 2 * ts + 1, D_POOL), F32),
                        pltpu.VMEM((KEYS_PAD, D_KV), F32),
                        pltpu.VMEM((KEYS_PAD, D_KV), F32),
                        pltpu.VMEM((Q_PER_KV * ts, HEAD_DIM), F32),
                        pltpu.VMEM((Q_PER_KV * ts, HEAD_DIM), F32)],
        compiler_params=pltpu.CompilerParams(dimension_semantics=("arbitrary",)),
        name="sample_mixers",
    )(sinks, proj, chist, phist, ck, cv, wconv, bconv.reshape(1, -1), lng.reshape(1, -1), lnb.reshape(1, -1),
      wpool, pscale.reshape(1, -1), qn.reshape(1, -1), kn.reshape(1, -1), cos_t, sin_lo, sin_hi)


def kernel(x_prompt, x_sample, state_conv, state_pool, cache_k, cache_v, norm_mix, w_in, w_conv, b_conv,
           ln_conv_g, ln_conv_b, w_pool, pool_scale, q_norm, k_norm, attn_sinks, w_out, norm_ffn, w_up, w_down):
    bp, sp, d_model = x_prompt.shape
    bs, ts, _ = x_sample.shape
    depth = w_in.shape[0]
    assert w_in.shape[2] == D_IN and sp % ATTN_BLOCK == 0 and KV_BUF == WINDOW
    mp, ms = bp * sp, bs * ts

    x = jnp.concatenate([x_prompt.reshape(mp, d_model), x_sample.reshape(ms, d_model)], axis=0)
    tab_p = _rope_tables(jnp.arange(sp, dtype=jnp.int32))
    tab_s = _rope_tables(PAST_LEN + jnp.arange(ts, dtype=jnp.int32))
    zero_conv = jnp.zeros((bp, CONV_BUF, D_CONV), F32)
    zero_pool = jnp.zeros((bp, POOL_BUF, D_POOL), F32)
    ck_all = cache_k.reshape(depth, bs, KV_BUF, D_KV)
    cv_all = cache_v.reshape(depth, bs, KV_BUF, D_KV)

    outs = [[] for _ in range(8)]
    for l in range(depth):
        mixer_w = (w_conv[l], b_conv[l], ln_conv_g[l], ln_conv_b[l], w_pool[l], pool_scale[l])

        proj = _wsmm(_rmsnorm(x, norm_mix[l]), w_in, l, name="in_proj")
        mix, conv_p, pool_p = _convpool_prompt(proj, zero_conv, zero_pool, *mixer_w, batch=bp, seq=sp)
        mix, k_p, v_p = _attn_prompt(proj, mix, attn_sinks[l], q_norm[l], k_norm[l], tab_p, batch=bp, seq=sp)
        mix_s, conv_s, pool_s, k_s, v_s = _sample_mixers(
            proj[mp:].reshape(bs, ts, D_IN), state_conv[l], state_pool[l], ck_all[l], cv_all[l],
            attn_sinks[l], *mixer_w, q_norm[l], k_norm[l], tab_s)
        mix = lax.dynamic_update_slice(mix, mix_s.reshape(ms, D_MIX).astype(BF16), (mp, 0))
        x = _wsmm(mix, w_out, l, res=x, tn=512, name="out_proj")

        up = _wsmm(_rmsnorm(x, norm_ffn[l]), w_up, l, relu2=True, out_dtype=BF16, name="ffn_up")
        x = _wsmm(up, w_down, l, res=x, tn=512, name="ffn_down")

        for lst, val in zip(outs, (conv_p, pool_p, k_p, v_p, conv_s, pool_s, k_s, v_s)):
            lst.append(val)

    conv_p, pool_p, k_p, v_p, conv_s, pool_s, k_s, v_s = [jnp.stack(o) for o in outs]
    kv_shape_p = (depth, bp, KV_BUF, N_KV_HEADS, HEAD_DIM)
    kv_shape_s = (depth, bs, KV_BUF, N_KV_HEADS, HEAD_DIM)
    return (x[:mp].reshape(bp, sp, d_model), x[mp:].reshape(bs, ts, d_model),
            conv_p, pool_p, k_p.reshape(kv_shape_p), v_p.reshape(kv_shape_p),
            conv_s, pool_s, k_s.reshape(kv_shape_s), v_s.reshape(kv_shape_s))
```

```python
import functools
import math

import jax
import jax.numpy as jnp
from jax import lax
from jax.experimental import pallas as pl
from jax.experimental.pallas import tpu as pltpu

F32 = jnp.float32
BF16 = jnp.bfloat16

HEAD_DIM = 128
N_KV_HEADS = 4
Q_PER_KV = 4
N_HEADS = N_KV_HEADS * Q_PER_KV
D_ATTN = N_HEADS * HEAD_DIM
D_KV = N_KV_HEADS * HEAD_DIM
D_CONV = 1024
D_POOL = 1024
D_MIX = D_CONV + D_POOL + D_ATTN
ROT_DIM = HEAD_DIM // 4
ROPE_THETA = 500000.0
WINDOW = 128
ATTN_BLOCK = 128
KV_BUF = 128
CONV_WIDTH = 31
CONV_BUF = CONV_WIDTH - 1
POOL_WINDOWS = (2, 4, 8, 16)
POOL_GROUP = D_POOL // len(POOL_WINDOWS)
POOL_BUF = max(POOL_WINDOWS) - 1
PAST_LEN = 8192
EPS = 1e-6

OFF_A = 0
OFF_B = D_CONV
OFF_Z = 2 * D_CONV
OFF_Q = 2 * D_CONV + D_POOL
OFF_K = OFF_Q + D_ATTN
OFF_V = OFF_K + D_KV
D_IN = OFF_V + D_KV

V7X_VMEM_BYTES = 64 * 1024 * 1024
LANES = 128
SUBLANES = 8
SUBLANES_BF16 = 16

ROPE_TM = 1040
ROPE_SUB = 5
RES_SUB = 1
UP_SUB = 1
RES_TM = 640
CONV_HALO = 32
POOL_HALO = 16


def _divisor_tile(n, target, multiple):
    best = None
    for d in range(multiple, min(n, target) + 1, multiple):
        if n % d == 0:
            best = d
    return n if best is None else best


def _vmem_limit(nbytes):
    return int(min(V7X_VMEM_BYTES - 2 * 1024 * 1024, nbytes + 6 * 1024 * 1024))


def _fold_lanes(sq):
    out = sq[:, 0:LANES]
    for c in range(1, sq.shape[1] // LANES):
        out = out + sq[:, c * LANES:(c + 1) * LANES]
    return out


def _row_rstd(ssq, d_norm):
    return lax.rsqrt(jnp.sum(ssq, axis=-1, keepdims=True) / d_norm + EPS)


def _prenorm_kernel(x_ref, g_ref, xg_ref, ssq_ref):
    x = x_ref[...]
    xg_ref[...] = (x * g_ref[...]).astype(xg_ref.dtype)
    ssq_ref[...] = _fold_lanes(x * x)


def _prenorm(x, g):
    m, d = x.shape
    tr = _divisor_tile(m, 384, SUBLANES_BF16)
    return pl.pallas_call(
        _prenorm_kernel,
        grid=(m // tr,),
        in_specs=[pl.BlockSpec((tr, d), lambda i: (i, 0)),
                  pl.BlockSpec((1, d), lambda i: (0, 0))],
        out_specs=[pl.BlockSpec((tr, d), lambda i: (i, 0)),
                   pl.BlockSpec((tr, LANES), lambda i: (i, 0))],
        out_shape=[jax.ShapeDtypeStruct((m, d), BF16),
                   jax.ShapeDtypeStruct((m, LANES), F32)],
        compiler_params=pltpu.CompilerParams(dimension_semantics=("arbitrary",)),
        name="prenorm",
    )(x, g.reshape(1, d))


def _head_norm_rope(acc, gain_ref, mask_ref, cos, sin_lo, sin_hi):
    half = ROT_DIM // 2
    cols = []
    for c in range(acc.shape[1] // HEAD_DIM):
        cs = slice(c * HEAD_DIM, (c + 1) * HEAD_DIM)
        a = acc[:, cs]
        y = a * lax.rsqrt(jnp.mean(a * a, axis=-1, keepdims=True) + EPS) * gain_ref[:, cs]
        up = pltpu.roll(y, HEAD_DIM - half, 1)
        down = pltpu.roll(y, half, 1)
        cols.append(jnp.where(mask_ref[:, cs] > 0.0, y * cos + up * sin_lo + down * sin_hi, a))
    return jnp.concatenate(cols, axis=1)


def _wsmm_kernel(*refs, tm, ck, nc, n_sub, relu2, has_res, has_scale, has_rope, emit_norm, d_norm):
    it = iter(refs)
    x_ref, w_ref = next(it), next(it)
    res_ref = next(it) if has_res else None
    ssq_in_ref = next(it) if has_scale else None
    rope_refs = [next(it) for _ in range(5)] if has_rope else None
    g_ref = next(it) if emit_norm else None
    o_ref = next(it)
    xg_ref, ssq_out_ref = (next(it), next(it)) if emit_norm else (None, None)
    wbf = next(it)
    ssq_acc = next(it) if emit_norm else None
    p = pl.program_id(0)
    i = pl.program_id(1)
    rows = pl.ds(pl.multiple_of(i * tm, tm), tm)

    def cast_chunk():
        wrows = pl.ds(pl.multiple_of(jnp.minimum(i, nc - 1) * ck, ck), ck)
        wbf[p % 2, wrows, :] = w_ref[...].astype(BF16)

    @pl.when(p == 0)
    def _():
        cast_chunk()
        if emit_norm:
            ssq_acc[rows, :] = jnp.zeros((tm, LANES), F32)

    def sub_tile(r):
        acc = jnp.dot(x_ref[r, :], wbf[(p + 1) % 2], preferred_element_type=F32)
        if has_scale:
            acc = acc * _row_rstd(ssq_in_ref[r, :], d_norm)
        if has_rope:
            gain_ref, mask_ref, cos_ref, slo_ref, shi_ref = rope_refs
            acc = _head_norm_rope(acc, gain_ref, mask_ref, cos_ref[r, :], slo_ref[r, :], shi_ref[r, :])
        if relu2:
            acc = jnp.square(jnp.maximum(acc, 0.0))
        if has_res:
            acc = res_ref[r, :] + acc
        o_ref[r, :] = acc.astype(o_ref.dtype)
        if emit_norm:
            xg_ref[r, :] = (acc * g_ref[...]).astype(xg_ref.dtype)
            acc_rows = pl.ds(pl.multiple_of(i * tm + r.start, SUBLANES_BF16), r.stop - r.start)
            total = ssq_acc[acc_rows, :] + _fold_lanes(acc * acc)
            ssq_acc[acc_rows, :] = total
            ssq_out_ref[r, :] = total

    @pl.when(p > 0)
    def _():
        sub = tm // n_sub
        for h in range(n_sub):
            sub_tile(slice(h * sub, (h + 1) * sub))
        cast_chunk()


def _wsmm(x, w, layer, *, k_chunk=0, res=None, ssq=None, rope=None, norm_gain=None, relu2=False, out_dtype=F32,
          tm_target=1040, n_sub=1, tn=1024, tk=4096, name="wsmm"):
    m = x.shape[0]
    n = w.shape[2]
    assert x.shape[1] % tk == 0 and w.shape[1] == x.shape[1] and n % tn == 0
    tm = _divisor_tile(m, tm_target, SUBLANES_BF16 * n_sub)
    ni, nj = m // tm, n // tn
    nc = max(c for c in range(1, ni + 1) if tk % c == 0 and (tk // c) % SUBLANES_BF16 == 0)
    ck = tk // nc
    has_res, has_scale, emit_norm = res is not None, ssq is not None, norm_gain is not None
    has_rope = rope is not None

    def x_map(p, i):
        return jnp.where(p == 0, 0, i), k_chunk

    def w_map(p, i):
        return layer, k_chunk * nc + jnp.where(p < nj, jnp.minimum(i, nc - 1), nc - 1), jnp.minimum(p, nj - 1)

    def o_map(p, i):
        return jnp.where(p == 0, 0, i), jnp.maximum(p - 1, 0)

    def row_map(p, i):
        return jnp.where(p == 0, 0, i), 0

    in_specs = [pl.BlockSpec((tm, tk), x_map), pl.BlockSpec((None, ck, tn), w_map)]
    args = [x, w]
    if has_res:
        in_specs.append(pl.BlockSpec((tm, tn), o_map))
        args.append(res)
    if has_scale:
        in_specs.append(pl.BlockSpec((tm, LANES), row_map))
        args.append(ssq)
    col_vec = pl.BlockSpec((1, tn), lambda p, i: (0, jnp.maximum(p - 1, 0)))
    if has_rope:
        gains, mask, cos_t, sin_lo, sin_hi = rope
        in_specs += [col_vec, col_vec] + [pl.BlockSpec((tm, HEAD_DIM), row_map)] * 3
        args += [gains.reshape(1, n), mask.reshape(1, n), cos_t, sin_lo, sin_hi]
    out_specs = [pl.BlockSpec((tm, tn), o_map)]
    out_shape = [jax.ShapeDtypeStruct((m, n), out_dtype)]
    scratch = [pltpu.VMEM((2, tk, tn), BF16)]
    if emit_norm:
        in_specs.append(col_vec)
        args.append(norm_gain.reshape(1, n))
        out_specs += [pl.BlockSpec((tm, tn), o_map),
                      pl.BlockSpec((tm, LANES), lambda p, i: (jnp.where(p == nj, i, 0), 0))]
        out_shape += [jax.ShapeDtypeStruct((m, n), BF16), jax.ShapeDtypeStruct((m, LANES), F32)]
        scratch.append(pltpu.VMEM((m, LANES), F32))
    out_bytes = jnp.dtype(out_dtype).itemsize
    vmem = 2 * (tm * tk * 2 + ck * tn * 4 + tm * tn * out_bytes) + 2 * tk * tn * 2
    vmem += 2 * tm * tn * 4 if has_res else 0
    vmem += 2 * tm * LANES * 4 if has_scale else 0
    vmem += 6 * tm * HEAD_DIM * 4 if has_rope else 0
    vmem += 2 * (tm * tn * 2 + tm * LANES * 4) + m * LANES * 4 if emit_norm else 0
    vmem += tm * tn * 4
    outs = pl.pallas_call(
        functools.partial(_wsmm_kernel, tm=tm, ck=ck, nc=nc, n_sub=n_sub, relu2=relu2, has_res=has_res, has_scale=has_scale,
                          has_rope=has_rope, emit_norm=emit_norm, d_norm=float(x.shape[1]) if has_scale else None),
        grid=(nj + 1, ni),
        in_specs=in_specs,
        out_specs=out_specs,
        out_shape=out_shape,
        scratch_shapes=scratch,
        compiler_params=pltpu.CompilerParams(
            dimension_semantics=("arbitrary", "arbitrary"),
            vmem_limit_bytes=_vmem_limit(vmem)),
        name=name,
    )(*args)
    return outs if emit_norm else outs[0]


def _sigmoid(x):
    return 1.0 / (1.0 + jnp.exp(-x))


def _layernorm_silu(c, g, b):
    mu = jnp.mean(c, axis=-1, keepdims=True)
    xc = c - mu
    var = jnp.mean(xc * xc, axis=-1, keepdims=True)
    y = xc * lax.rsqrt(var + EPS) * g + b
    return y * _sigmoid(y)


def _rope_tables(pos):
    half = ROT_DIM // 2
    inv_freq = jnp.exp(-math.log(ROPE_THETA) * 2.0 * jnp.arange(half, dtype=F32) / ROT_DIM)
    ang = pos.astype(F32)[:, None] * inv_freq[None, :]
    cos, sin = jnp.cos(ang), jnp.sin(ang)
    s = pos.shape[0]
    ones = jnp.ones((s, HEAD_DIM - ROT_DIM), F32)
    zeros = jnp.zeros((s, HEAD_DIM - ROT_DIM), F32)
    zh = jnp.zeros((s, half), F32)
    cos_t = jnp.concatenate([cos, cos, ones], axis=-1)
    sin_lo = jnp.concatenate([-sin, zh, zeros], axis=-1)
    sin_hi = jnp.concatenate([zh, sin, zeros], axis=-1)
    return cos_t, sin_lo, sin_hi


def _pool_group(zbuf, base, rows, g, w, pos0, wpool_ref, pscale_ref):
    ls = slice(g * POOL_GROUP, (g + 1) * POOL_GROUP)
    cur = zbuf[base:base + rows, ls]
    s = cur
    for i in range(1, w):
        s = s + zbuf[base - i:base - i + rows, ls]
    pos = pos0 + lax.broadcasted_iota(jnp.int32, (rows, 1), 0)
    cnt = jnp.minimum(pos + 1, w).astype(F32)
    p = (s / cnt - cur).astype(BF16)
    y = jnp.dot(p, wpool_ref[g].astype(BF16), preferred_element_type=F32)
    return y * pscale_ref[:, ls]


def _convpool_kernel(proj_ref, chist_ref, phist_ref, wconv_ref, bconv_ref, lng_ref, lnb_ref,
                     wpool_ref, pscale_ref, mix_ref, cnew_ref, pnew_ref, ubuf, ush, zbuf, cbuf,
                     *, tile, n_tiles):
    t = pl.program_id(1)

    @pl.when(t == 0)
    def _():
        ubuf[0:CONV_HALO - CONV_BUF, :] = jnp.zeros((CONV_HALO - CONV_BUF, D_CONV), F32)
        ubuf[CONV_HALO - CONV_BUF:CONV_HALO, :] = chist_ref[0]
        zbuf[0:POOL_HALO - POOL_BUF, :] = jnp.zeros((POOL_HALO - POOL_BUF, D_POOL), F32)
        zbuf[POOL_HALO - POOL_BUF:POOL_HALO, :] = phist_ref[0]

    a = proj_ref[:, OFF_A:OFF_A + D_CONV]
    b = proj_ref[:, OFF_B:OFF_B + D_CONV]
    ubuf[CONV_HALO:CONV_HALO + tile, :] = a * _sigmoid(b)
    zbuf[POOL_HALO:POOL_HALO + tile, :] = proj_ref[:, OFF_Z:OFF_Z + D_POOL]

    span = tile + CONV_HALO - SUBLANES
    for s in range(1, SUBLANES):
        ush[s - 1, 0:span, :] = ubuf[s:s + span, :]
    rc, lc = 64, 256
    first = CONV_HALO - CONV_BUF
    for r0 in range(0, tile, rc):
        for c in range(D_CONV // lc):
            ls = slice(c * lc, (c + 1) * lc)
            acc = None
            for k in range(CONV_WIDTH):
                q, s = divmod(first + k, SUBLANES)
                rows = slice(r0 + q * SUBLANES, r0 + q * SUBLANES + rc)
                src = ubuf[rows, ls] if s == 0 else ush[s - 1, rows, ls]
                term = src * wconv_ref[k:k + 1, ls]
                acc = term if acc is None else acc + term
            cbuf[r0:r0 + rc, ls] = acc + bconv_ref[:, ls]
    c = _layernorm_silu(cbuf[...], lng_ref[...], lnb_ref[...])
    mix_ref[:, 0:D_CONV] = c.astype(mix_ref.dtype)

    for g, w in enumerate(POOL_WINDOWS):
        y = _pool_group(zbuf, POOL_HALO, tile, g, w, t * tile, wpool_ref, pscale_ref)
        mix_ref[:, D_CONV + g * POOL_GROUP:D_CONV + (g + 1) * POOL_GROUP] = y.astype(mix_ref.dtype)

    @pl.when(t == n_tiles - 1)
    def _():
        cnew_ref[0] = ubuf[CONV_HALO + tile - CONV_BUF:CONV_HALO + tile, :]
        pnew_ref[0] = zbuf[POOL_HALO + tile - POOL_BUF:POOL_HALO + tile, :]

    ubuf[0:CONV_HALO, :] = ubuf[tile:tile + CONV_HALO, :]
    zbuf[0:POOL_HALO, :] = zbuf[tile:tile + POOL_HALO, :]


def _convpool_prompt(proj, chist, phist, wconv, bconv, lng, lnb, wpool, pscale, *, batch, seq):
    tile = _divisor_tile(seq, 256, 2 * CONV_HALO)
    n_tiles = seq // tile
    m = proj.shape[0]
    row = lambda b, t: (b * n_tiles + t, 0)
    const2 = lambda b, t: (0, 0)
    per_batch = lambda b, t: (b, 0, 0)
    width = OFF_Q
    return pl.pallas_call(
        functools.partial(_convpool_kernel, tile=tile, n_tiles=n_tiles),
        grid=(batch, n_tiles),
        in_specs=[pl.BlockSpec((tile, width), row),
                  pl.BlockSpec((1, CONV_BUF, D_CONV), per_batch),
                  pl.BlockSpec((1, POOL_BUF, D_POOL), per_batch),
                  pl.BlockSpec((CONV_WIDTH, D_CONV), const2),
                  pl.BlockSpec((1, D_CONV), const2),
                  pl.BlockSpec((1, D_CONV), const2),
                  pl.BlockSpec((1, D_CONV), const2),
                  pl.BlockSpec((len(POOL_WINDOWS), POOL_GROUP, POOL_GROUP), lambda b, t: (0, 0, 0)),
                  pl.BlockSpec((1, D_POOL), const2)],
        out_specs=[pl.BlockSpec((tile, D_CONV + D_POOL), row),
                   pl.BlockSpec((1, CONV_BUF, D_CONV), per_batch),
                   pl.BlockSpec((1, POOL_BUF, D_POOL), per_batch)],
        out_shape=[jax.ShapeDtypeStruct((m, D_MIX), BF16),
                   jax.ShapeDtypeStruct((batch, CONV_BUF, D_CONV), F32),
                   jax.ShapeDtypeStruct((batch, POOL_BUF, D_POOL), F32)],
        scratch_shapes=[pltpu.VMEM((CONV_HALO + tile, D_CONV), F32),
                        pltpu.VMEM((SUBLANES - 1, CONV_HALO + tile, D_CONV), F32),
                        pltpu.VMEM((POOL_HALO + tile, D_POOL), F32),
                        pltpu.VMEM((tile, D_CONV), F32)],
        compiler_params=pltpu.CompilerParams(dimension_semantics=("arbitrary", "arbitrary")),
        name="convpool_prompt",
    )(proj, chist, phist, wconv, bconv.reshape(1, -1), lng.reshape(1, -1), lnb.reshape(1, -1),
      wpool, pscale.reshape(1, -1))


BLOCK_UNROLL = 4


def _softmax_sink_pv(s, sink_col, vv):
    m = jnp.maximum(jnp.max(s, axis=-1, keepdims=True), sink_col)
    e = jnp.exp(s - m)
    den = jnp.sum(e, axis=-1, keepdims=True) + jnp.exp(sink_col - m)
    return jnp.dot(e.astype(BF16), vv, preferred_element_type=F32) / den


def _attn_prompt_kernel(sink_ref, q_ref, k_ref, v_ref, mix_in_ref, o_ref, knew_ref, vnew_ref, kbuf, vbuf, bias,
                        *, seq):
    del mix_in_ref
    kh = pl.program_id(1)
    blk = ATTN_BLOCK
    n_blocks = seq // blk
    rows_all = Q_PER_KV * blk

    kbuf[0:blk, :] = jnp.zeros((blk, HEAD_DIM), BF16)
    vbuf[0:blk, :] = jnp.zeros((blk, HEAD_DIM), BF16)

    row = lax.broadcasted_iota(jnp.int32, (rows_all, 1), 0)
    qi = jnp.bitwise_and(row, blk - 1)
    kj = lax.broadcasted_iota(jnp.int32, (1, 2 * blk), 1)
    band = (kj >= qi) & (kj <= qi + WINDOW)
    bias[1] = jnp.where(band, 0.0, -jnp.inf)
    bias[0] = jnp.where(band & (kj >= blk), 0.0, -jnp.inf)
    sink_col = jnp.zeros((rows_all, 1), F32)
    for g in range(Q_PER_KV):
        sink_col = jnp.where((row >= g * blk) & (row < (g + 1) * blk), sink_ref[kh * Q_PER_KV + g], sink_col)
    scale = HEAD_DIM ** -0.5

    last = slice(seq - KV_BUF, seq)
    knew_ref[0] = k_ref[last, :]
    vnew_ref[0] = v_ref[last, :]
    kbuf[blk:blk + seq, :] = k_ref[...].astype(BF16)
    vbuf[blk:blk + seq, :] = v_ref[...].astype(BF16)

    def block(n, carry):
        r0 = pl.multiple_of(n * blk, blk)
        rows = pl.ds(r0, blk)
        q4 = jnp.concatenate(
            [q_ref[rows, g * HEAD_DIM:(g + 1) * HEAD_DIM].astype(BF16) for g in range(Q_PER_KV)], axis=0)
        kk = kbuf[pl.ds(r0, 2 * blk), :]
        vv = vbuf[pl.ds(r0, 2 * blk), :]
        s = lax.dot_general(q4, kk, (((1,), (1,)), ((), ())), preferred_element_type=F32) * scale
        s = s + bias[jnp.minimum(n, 1)]
        o = _softmax_sink_pv(s, sink_col, vv)
        for g in range(Q_PER_KV):
            o_ref[rows, g * HEAD_DIM:(g + 1) * HEAD_DIM] = o[g * blk:(g + 1) * blk, :].astype(o_ref.dtype)
        return carry

    lax.fori_loop(0, n_blocks, block, 0, unroll=BLOCK_UNROLL)


def _attn_prompt(proj, mix, sinks, *, batch, seq):
    m = proj.shape[0]
    gw = Q_PER_KV * HEAD_DIM
    return pl.pallas_call(
        functools.partial(_attn_prompt_kernel, seq=seq),
        grid=(batch, N_KV_HEADS),
        in_specs=[pl.BlockSpec(memory_space=pltpu.SMEM),
                  pl.BlockSpec((seq, gw), lambda b, h: (b, OFF_Q // gw + h)),
                  pl.BlockSpec((seq, HEAD_DIM), lambda b, h: (b, OFF_K // HEAD_DIM + h)),
                  pl.BlockSpec((seq, HEAD_DIM), lambda b, h: (b, OFF_V // HEAD_DIM + h)),
                  pl.BlockSpec(memory_space=pl.ANY)],
        out_specs=[pl.BlockSpec((seq, gw), lambda b, h: (b, (D_CONV + D_POOL) // gw + h)),
                   pl.BlockSpec((1, KV_BUF, HEAD_DIM), lambda b, h: (b, 0, h)),
                   pl.BlockSpec((1, KV_BUF, HEAD_DIM), lambda b, h: (b, 0, h))],
        out_shape=[jax.ShapeDtypeStruct((m, D_MIX), BF16),
                   jax.ShapeDtypeStruct((batch, KV_BUF, D_KV), F32),
                   jax.ShapeDtypeStruct((batch, KV_BUF, D_KV), F32)],
        scratch_shapes=[pltpu.VMEM((seq + ATTN_BLOCK, HEAD_DIM), BF16),
                        pltpu.VMEM((seq + ATTN_BLOCK, HEAD_DIM), BF16),
                        pltpu.VMEM((2, Q_PER_KV * ATTN_BLOCK, 2 * ATTN_BLOCK), F32)],
        input_output_aliases={4: 0},
        compiler_params=pltpu.CompilerParams(dimension_semantics=("arbitrary", "arbitrary")),
        name="attn_prompt",
    )(sinks, proj, proj, proj, mix)


KEYS_PAD = 256


def _sample_one(bb, row0, sink_ref, proj_ref, chist_ref, phist_ref, ck_ref, cv_ref, wconv_ref, bconv_ref,
                lng_ref, lnb_ref, wpool_ref, pscale_ref,
                cnew_ref, pnew_ref, knew_ref, vnew_ref, fbuf, pbuf, kkbuf, vvbuf, qbuf, obuf, mstep, ts):
    out_rows = slice(row0, row0 + ts)
    a = proj_ref[bb, :, OFF_A:OFF_A + D_CONV]
    b = proj_ref[bb, :, OFF_B:OFF_B + D_CONV]
    fbuf[0:CONV_BUF, :] = chist_ref[bb]
    fbuf[CONV_BUF:CONV_BUF + ts, :] = a * _sigmoid(b)
    acc = fbuf[0:ts, :] * wconv_ref[0:1, :]
    for k in range(1, CONV_WIDTH):
        acc = acc + fbuf[k:k + ts, :] * wconv_ref[k:k + 1, :]
    mstep[out_rows, 0:D_CONV] = _layernorm_silu(acc + bconv_ref[...], lng_ref[...], lnb_ref[...])
    cnew_ref[bb] = fbuf[ts:ts + CONV_BUF, :]

    pbuf[0:POOL_BUF, :] = phist_ref[bb]
    pbuf[POOL_BUF:POOL_BUF + ts, :] = proj_ref[bb, :, OFF_Z:OFF_Z + D_POOL]
    for g, w in enumerate(POOL_WINDOWS):
        y = _pool_group(pbuf, POOL_BUF, ts, g, w, PAST_LEN, wpool_ref, pscale_ref)
        mstep[out_rows, D_CONV + g * POOL_GROUP:D_CONV + (g + 1) * POOL_GROUP] = y
    pnew_ref[bb] = pbuf[ts:ts + POOL_BUF, :]

    kkbuf[0:KV_BUF, :] = ck_ref[bb]
    vvbuf[0:KV_BUF, :] = cv_ref[bb]
    kkbuf[KV_BUF:KV_BUF + ts, :] = proj_ref[bb, :, OFF_K:OFF_K + D_KV]
    vvbuf[KV_BUF:KV_BUF + ts, :] = proj_ref[bb, :, OFF_V:OFF_V + D_KV]
    pad = KEYS_PAD - KV_BUF - ts
    kkbuf[KV_BUF + ts:KEYS_PAD, :] = jnp.zeros((pad, D_KV), F32)
    vvbuf[KV_BUF + ts:KEYS_PAD, :] = jnp.zeros((pad, D_KV), F32)
    knew_ref[bb] = kkbuf[ts:ts + KV_BUF, :]
    vnew_ref[bb] = vvbuf[ts:ts + KV_BUF, :]

    rows_all = Q_PER_KV * ts
    row = lax.broadcasted_iota(jnp.int32, (rows_all, 1), 0)
    tq = row % ts
    kj = lax.broadcasted_iota(jnp.int32, (1, KEYS_PAD), 1)
    valid = (kj >= tq + (KV_BUF - WINDOW)) & (kj <= tq + KV_BUF) & (kj < KV_BUF + ts)
    scale = HEAD_DIM ** -0.5
    for kh in range(N_KV_HEADS):
        hs = slice(kh * HEAD_DIM, (kh + 1) * HEAD_DIM)
        sink_col = jnp.zeros((rows_all, 1), F32)
        for g in range(Q_PER_KV):
            h = kh * Q_PER_KV + g
            qbuf[g * ts:(g + 1) * ts, :] = proj_ref[bb, :, OFF_Q + h * HEAD_DIM:OFF_Q + (h + 1) * HEAD_DIM]
            sink_col = jnp.where((row >= g * ts) & (row < (g + 1) * ts), sink_ref[h], sink_col)
        q = qbuf[...].astype(BF16)
        kk = kkbuf[:, hs].astype(BF16)
        vv = vvbuf[:, hs].astype(BF16)
        s = lax.dot_general(q, kk, (((1,), (1,)), ((), ())), preferred_element_type=F32) * scale
        s = jnp.where(valid, s, -jnp.inf)
        obuf[...] = _softmax_sink_pv(s, sink_col, vv)
        for g in range(Q_PER_KV):
            h = kh * Q_PER_KV + g
            col = D_CONV + D_POOL + h * HEAD_DIM
            mstep[out_rows, col:col + HEAD_DIM] = obuf[g * ts:(g + 1) * ts, :]


def _sample_kernel(*refs, steps, seqs, n_steps, n_alias):
    (sink_ref, proj_ref, chist_ref, phist_ref, ck_ref, cv_ref, wconv_ref, bconv_ref,
     lng_ref, lnb_ref, wpool_ref, pscale_ref) = refs[:12]
    (mix_ref, cnew_ref, pnew_ref, knew_ref, vnew_ref, fbuf, pbuf, kkbuf, vvbuf, qbuf, obuf,
     mstep, mixacc) = refs[12 + n_alias:]
    step = pl.program_id(0)
    for bb in range(seqs):
        _sample_one(bb, bb * steps, sink_ref, proj_ref, chist_ref, phist_ref, ck_ref, cv_ref, wconv_ref,
                    bconv_ref, lng_ref, lnb_ref, wpool_ref, pscale_ref,
                    cnew_ref, pnew_ref, knew_ref, vnew_ref, fbuf.at[bb], pbuf.at[bb], kkbuf.at[bb], vvbuf.at[bb],
                    qbuf.at[bb], obuf.at[bb], mstep, steps)
    rows = seqs * steps
    mixacc[pl.ds(pl.multiple_of(step * rows, rows), rows), :] = mstep[...]

    @pl.when(step == n_steps - 1)
    def _():
        mix_ref[...] = mixacc[...].astype(mix_ref.dtype)


def _sample_mixers(proj, mix, chist, phist, ck, cv, sinks, wconv, bconv, lng, lnb, wpool, pscale,
                   *, row0, layer, depth, prev_states):
    nb, ts, _ = proj.shape
    assert SUBLANES % ts == 0
    seqs = SUBLANES // ts
    assert nb % seqs == 0 and row0 % (nb * ts) == 0 and (nb * ts) % SUBLANES_BF16 == 0
    n_steps = nb // seqs
    per_step = lambda s: (s, 0, 0)
    const2 = lambda s: (0, 0)
    vec = lambda n: pl.BlockSpec((1, n), const2)
    stacked = lambda s: (layer, s, 0, 0)
    aliased = [mix] + list(prev_states or ())
    n_fixed = 12
    return pl.pallas_call(
        functools.partial(_sample_kernel, steps=ts, seqs=seqs, n_steps=n_steps, n_alias=len(aliased)),
        grid=(n_steps,),
        in_specs=[pl.BlockSpec(memory_space=pltpu.SMEM),
                  pl.BlockSpec((seqs, ts, D_IN), per_step),
                  pl.BlockSpec((seqs, CONV_BUF, D_CONV), per_step),
                  pl.BlockSpec((seqs, POOL_BUF, D_POOL), per_step),
                  pl.BlockSpec((seqs, KV_BUF, D_KV), per_step),
                  pl.BlockSpec((seqs, KV_BUF, D_KV), per_step),
                  pl.BlockSpec((CONV_WIDTH, D_CONV), const2),
                  vec(D_CONV), vec(D_CONV), vec(D_CONV),
                  pl.BlockSpec((len(POOL_WINDOWS), POOL_GROUP, POOL_GROUP), lambda s: (0, 0, 0)),
                  vec(D_POOL)] + [pl.BlockSpec(memory_space=pl.ANY)] * len(aliased),
        out_specs=[pl.BlockSpec((nb * ts, D_MIX), lambda s: (row0 // (nb * ts), 0)),
                   pl.BlockSpec((None, seqs, CONV_BUF, D_CONV), stacked),
                   pl.BlockSpec((None, seqs, POOL_BUF, D_POOL), stacked),
                   pl.BlockSpec((None, seqs, KV_BUF, D_KV), stacked),
                   pl.BlockSpec((None, seqs, KV_BUF, D_KV), stacked)],
        out_shape=[jax.ShapeDtypeStruct(mix.shape, mix.dtype),
                   jax.ShapeDtypeStruct((depth, nb, CONV_BUF, D_CONV), F32),
                   jax.ShapeDtypeStruct((depth, nb, POOL_BUF, D_POOL), F32),
                   jax.ShapeDtypeStruct((depth, nb, KV_BUF, D_KV), F32),
                   jax.ShapeDtypeStruct((depth, nb, KV_BUF, D_KV), F32)],
        scratch_shapes=[pltpu.VMEM((seqs, CONV_BUF + 2 * ts + 2, D_CONV), F32),
                        pltpu.VMEM((seqs, POOL_BUF + 2 * ts + 1, D_POOL), F32),
                        pltpu.VMEM((seqs, KEYS_PAD, D_KV), F32),
                        pltpu.VMEM((seqs, KEYS_PAD, D_KV), F32),
                        pltpu.VMEM((seqs, Q_PER_KV * ts, HEAD_DIM), F32),
                        pltpu.VMEM((seqs, Q_PER_KV * ts, HEAD_DIM), F32),
                        pltpu.VMEM((seqs * ts, D_MIX), F32),
                        pltpu.VMEM((nb * ts, D_MIX), F32)],
        input_output_aliases={n_fixed + a: a for a in range(len(aliased))},
        compiler_params=pltpu.CompilerParams(dimension_semantics=("arbitrary",)),
        name="sample_mixers",
    )(sinks, proj, chist, phist, ck, cv, wconv, bconv.reshape(1, -1), lng.reshape(1, -1), lnb.reshape(1, -1),
      wpool, pscale.reshape(1, -1), *aliased)


def kernel(x_prompt, x_sample, state_conv, state_pool, cache_k, cache_v, norm_mix, w_in, w_conv, b_conv,
           ln_conv_g, ln_conv_b, w_pool, pool_scale, q_norm, k_norm, attn_sinks, w_out, norm_ffn, w_up, w_down):
    bp, sp, d_model = x_prompt.shape
    bs, ts, _ = x_sample.shape
    depth = w_in.shape[0]
    assert w_in.shape[2] == D_IN and sp % ATTN_BLOCK == 0 and KV_BUF == WINDOW
    mp, ms = bp * sp, bs * ts

    x = jnp.concatenate([x_prompt.reshape(mp, d_model), x_sample.reshape(ms, d_model)], axis=0)
    tab_p = _rope_tables(jnp.arange(sp, dtype=jnp.int32))
    tab_s = _rope_tables(PAST_LEN + jnp.arange(ts, dtype=jnp.int32))
    tables = tuple(jnp.concatenate([jnp.tile(tp, (bp, 1)), jnp.tile(tsm, (bs, 1))], axis=0)
                   for tp, tsm in zip(tab_p, tab_s))
    head_mask = jnp.zeros((D_IN,), F32).at[OFF_Q:OFF_V].set(1.0)
    zero_conv = jnp.zeros((bp, CONV_BUF, D_CONV), F32)
    zero_pool = jnp.zeros((bp, POOL_BUF, D_POOL), F32)
    ck_all = cache_k.reshape(depth, bs, KV_BUF, D_KV)
    cv_all = cache_v.reshape(depth, bs, KV_BUF, D_KV)

    n_chunks = w_up.shape[2] // d_model

    outs = [[] for _ in range(4)]
    states_s = None
    xg, ssq = _prenorm(x, norm_mix[0])
    for l in range(depth):
        mixer_w = (w_conv[l], b_conv[l], ln_conv_g[l], ln_conv_b[l], w_pool[l], pool_scale[l])
        head_gain = jnp.concatenate([jnp.ones((OFF_Q,), F32), jnp.tile(q_norm[l], N_HEADS),
                                     jnp.tile(k_norm[l], N_KV_HEADS), jnp.ones((D_KV,), F32)])

        proj = _wsmm(xg, w_in, l, ssq=ssq, rope=(head_gain, head_mask) + tables, tm_target=ROPE_TM, n_sub=ROPE_SUB,
                     name="in_proj")
        mix, conv_p, pool_p = _convpool_prompt(proj, zero_conv, zero_pool, *mixer_w, batch=bp, seq=sp)
        mix, k_p, v_p = _attn_prompt(proj, mix, attn_sinks[l], batch=bp, seq=sp)
        mix, *states_s = _sample_mixers(
            proj[mp:].reshape(bs, ts, D_IN), mix, state_conv[l], state_pool[l], ck_all[l], cv_all[l],
            attn_sinks[l], *mixer_w, row0=mp, layer=l, depth=depth, prev_states=states_s)
        res_tiles = dict(tm_target=RES_TM, n_sub=RES_SUB)
        x, xg, ssq = _wsmm(mix, w_out, l, res=x, norm_gain=norm_ffn[l], name="out_proj", **res_tiles)

        up = _wsmm(xg, w_up, l, ssq=ssq, relu2=True, out_dtype=BF16, n_sub=UP_SUB, name="ffn_up")
        for c in range(n_chunks):
            if c == n_chunks - 1 and l + 1 < depth:
                x, xg, ssq = _wsmm(up, w_down, l, k_chunk=c, res=x, norm_gain=norm_mix[l + 1], name="ffn_down",
                                   **res_tiles)
            else:
                x = _wsmm(up, w_down, l, k_chunk=c, res=x, name="ffn_down", **res_tiles)

        for lst, val in zip(outs, (conv_p, pool_p, k_p, v_p)):
            lst.append(val)

    conv_p, pool_p, k_p, v_p = [jnp.stack(o) for o in outs]
    conv_s, pool_s, k_s, v_s = states_s
    kv_shape_p = (depth, bp, KV_BUF, N_KV_HEADS, HEAD_DIM)
    kv_shape_s = (depth, bs, KV_BUF, N_KV_HEADS, HEAD_DIM)
    return (x[:mp].reshape(bp, sp, d_model), x[mp:].reshape(bs, ts, d_model),
            conv_p, pool_p, k_p.reshape(kv_shape_p), v_p.reshape(kv_shape_p),
            conv_s, pool_s, k_s.reshape(kv_shape_s), v_s.reshape(kv_shape_s))
```

```python
import functools
import math

import jax
import jax.numpy as jnp
from jax import lax
from jax.experimental import pallas as pl
from jax.experimental.pallas import tpu as pltpu

F32 = jnp.float32
BF16 = jnp.bfloat16

HEAD_DIM = 128
N_KV_HEADS = 4
Q_PER_KV = 4
N_HEADS = N_KV_HEADS * Q_PER_KV
D_ATTN = N_HEADS * HEAD_DIM
D_KV = N_KV_HEADS * HEAD_DIM
D_CONV = 1024
D_POOL = 1024
D_MIX = D_CONV + D_POOL + D_ATTN
ROT_DIM = HEAD_DIM // 4
ROPE_THETA = 500000.0
WINDOW = 128
ATTN_BLOCK = 128
KV_BUF = 128
CONV_WIDTH = 31
CONV_BUF = CONV_WIDTH - 1
POOL_WINDOWS = (2, 4, 8, 16)
POOL_GROUP = D_POOL // len(POOL_WINDOWS)
POOL_BUF = max(POOL_WINDOWS) - 1
PAST_LEN = 8192
EPS = 1e-6

OFF_A = 0
OFF_B = D_CONV
OFF_Z = 2 * D_CONV
OFF_Q = 2 * D_CONV + D_POOL
OFF_K = OFF_Q + D_ATTN
OFF_V = OFF_K + D_KV
D_IN = OFF_V + D_KV

V7X_VMEM_BYTES = 64 * 1024 * 1024
LANES = 128
SUBLANES = 8
SUBLANES_BF16 = 16

ROPE_TM = 1040
ROPE_SUB = 5
RES_TM = 832
NORM_TM = 640
CONV_HALO = 32
POOL_HALO = 16


def _divisor_tile(n, target, multiple):
    best = None
    for d in range(multiple, min(n, target) + 1, multiple):
        if n % d == 0:
            best = d
    return n if best is None else best


def _vmem_limit(nbytes):
    return int(min(V7X_VMEM_BYTES - 2 * 1024 * 1024, nbytes + 6 * 1024 * 1024))


def _fold_lanes(sq):
    out = sq[:, 0:LANES]
    for c in range(1, sq.shape[1] // LANES):
        out = out + sq[:, c * LANES:(c + 1) * LANES]
    return out


def _row_rstd(ssq, d_norm):
    return lax.rsqrt(jnp.sum(ssq, axis=-1, keepdims=True) / d_norm + EPS)


def _merge_prenorm_kernel(xp_ref, xs_ref, g_ref, x_ref, xg_ref, ssq_ref, *, n_prompt_tiles):
    x = jnp.where(pl.program_id(0) < n_prompt_tiles, xp_ref[...], xs_ref[...])
    x_ref[...] = x
    xg_ref[...] = (x * g_ref[...]).astype(xg_ref.dtype)
    ssq_ref[...] = _fold_lanes(x * x)


def _merge_prenorm(x_prompt, x_sample, g):
    (mp, d), ms = x_prompt.shape, x_sample.shape[0]
    assert mp % ms == 0 and ms % SUBLANES_BF16 == 0
    npt = mp // ms
    m = mp + ms
    row = lambda i: (i, 0)
    return pl.pallas_call(
        functools.partial(_merge_prenorm_kernel, n_prompt_tiles=npt),
        grid=(npt + 1,),
        in_specs=[pl.BlockSpec((ms, d), lambda i: (jnp.minimum(i, npt - 1), 0)),
                  pl.BlockSpec((ms, d), lambda i: (0, 0)),
                  pl.BlockSpec((1, d), lambda i: (0, 0))],
        out_specs=[pl.BlockSpec((ms, d), row), pl.BlockSpec((ms, d), row), pl.BlockSpec((ms, LANES), row)],
        out_shape=[jax.ShapeDtypeStruct((m, d), F32),
                   jax.ShapeDtypeStruct((m, d), BF16),
                   jax.ShapeDtypeStruct((m, LANES), F32)],
        compiler_params=pltpu.CompilerParams(dimension_semantics=("arbitrary",)),
        name="merge_prenorm",
    )(x_prompt, x_sample, g.reshape(1, d))


def _head_norm_rope(acc, gain_ref, mask_ref, cos, sin_lo, sin_hi):
    half = ROT_DIM // 2
    cols = []
    for c in range(acc.shape[1] // HEAD_DIM):
        cs = slice(c * HEAD_DIM, (c + 1) * HEAD_DIM)
        a = acc[:, cs]
        y = a * lax.rsqrt(jnp.mean(a * a, axis=-1, keepdims=True) + EPS) * gain_ref[:, cs]
        up = pltpu.roll(y, HEAD_DIM - half, 1)
        down = pltpu.roll(y, half, 1)
        cols.append(jnp.where(mask_ref[:, cs] > 0.0, y * cos + up * sin_lo + down * sin_hi, a))
    return jnp.concatenate(cols, axis=1)


def _wsmm_kernel(*refs, tm, ck, nc, n_sub, relu2, has_res, has_scale, has_rope, emit_norm, d_norm):
    it = iter(refs)
    x_ref, w_ref = next(it), next(it)
    res_ref = next(it) if has_res else None
    ssq_in_ref = next(it) if has_scale else None
    rope_refs = [next(it) for _ in range(5)] if has_rope else None
    g_ref = next(it) if emit_norm else None
    o_ref = next(it)
    xg_ref, ssq_out_ref = (next(it), next(it)) if emit_norm else (None, None)
    wbf = next(it)
    ssq_acc = next(it) if emit_norm else None
    p = pl.program_id(0)
    i = pl.program_id(1)
    rows = pl.ds(pl.multiple_of(i * tm, tm), tm)

    def cast_chunk():
        wrows = pl.ds(pl.multiple_of(jnp.minimum(i, nc - 1) * ck, ck), ck)
        wbf[p % 2, wrows, :] = w_ref[...].astype(BF16)

    @pl.when(p == 0)
    def _():
        cast_chunk()
        if emit_norm:
            ssq_acc[rows, :] = jnp.zeros((tm, LANES), F32)

    def sub_tile(r):
        acc = jnp.dot(x_ref[r, :], wbf[(p + 1) % 2], preferred_element_type=F32)
        if has_scale:
            acc = acc * _row_rstd(ssq_in_ref[r, :], d_norm)
        if has_rope:
            gain_ref, mask_ref, cos_ref, slo_ref, shi_ref = rope_refs
            acc = _head_norm_rope(acc, gain_ref, mask_ref, cos_ref[r, :], slo_ref[r, :], shi_ref[r, :])
        if relu2:
            acc = jnp.square(jnp.maximum(acc, 0.0))
        if has_res:
            acc = res_ref[r, :] + acc
        o_ref[r, :] = acc.astype(o_ref.dtype)
        if emit_norm:
            xg_ref[r, :] = (acc * g_ref[...]).astype(xg_ref.dtype)
            acc_rows = pl.ds(pl.multiple_of(i * tm + r.start, SUBLANES_BF16), r.stop - r.start)
            total = ssq_acc[acc_rows, :] + _fold_lanes(acc * acc)
            ssq_acc[acc_rows, :] = total
            ssq_out_ref[r, :] = total

    @pl.when(p > 0)
    def _():
        sub = tm // n_sub
        for h in range(n_sub):
            sub_tile(slice(h * sub, (h + 1) * sub))
        cast_chunk()


def _wsmm(x, w, layer, *, k_chunk=0, res=None, ssq=None, rope=None, norm_gain=None, relu2=False, out_dtype=F32,
          tm_target=1040, n_sub=1, tn=1024, tk=4096, name="wsmm"):
    m = x.shape[0]
    n = w.shape[2]
    assert x.shape[1] % tk == 0 and w.shape[1] == x.shape[1] and n % tn == 0
    tm = _divisor_tile(m, tm_target, SUBLANES_BF16 * n_sub)
    ni, nj = m // tm, n // tn
    nc = max(c for c in range(1, ni + 1) if tk % c == 0 and (tk // c) % SUBLANES_BF16 == 0)
    ck = tk // nc
    has_res, has_scale, emit_norm = res is not None, ssq is not None, norm_gain is not None
    has_rope = rope is not None

    def x_map(p, i):
        return jnp.where(p == 0, 0, i), k_chunk

    def w_map(p, i):
        return layer, k_chunk * nc + jnp.where(p < nj, jnp.minimum(i, nc - 1), nc - 1), jnp.minimum(p, nj - 1)

    def o_map(p, i):
        return jnp.where(p == 0, 0, i), jnp.maximum(p - 1, 0)

    def row_map(p, i):
        return jnp.where(p == 0, 0, i), 0

    in_specs = [pl.BlockSpec((tm, tk), x_map), pl.BlockSpec((None, ck, tn), w_map)]
    args = [x, w]
    if has_res:
        in_specs.append(pl.BlockSpec((tm, tn), o_map))
        args.append(res)
    if has_scale:
        in_specs.append(pl.BlockSpec((tm, LANES), row_map))
        args.append(ssq)
    col_vec = pl.BlockSpec((1, tn), lambda p, i: (0, jnp.maximum(p - 1, 0)))
    if has_rope:
        gains, mask, cos_t, sin_lo, sin_hi = rope
        in_specs += [col_vec, col_vec] + [pl.BlockSpec((tm, HEAD_DIM), row_map)] * 3
        args += [gains.reshape(1, n), mask.reshape(1, n), cos_t, sin_lo, sin_hi]
    out_specs = [pl.BlockSpec((tm, tn), o_map)]
    out_shape = [jax.ShapeDtypeStruct((m, n), out_dtype)]
    scratch = [pltpu.VMEM((2, tk, tn), BF16)]
    if emit_norm:
        in_specs.append(col_vec)
        args.append(norm_gain.reshape(1, n))
        out_specs += [pl.BlockSpec((tm, tn), o_map),
                      pl.BlockSpec((tm, LANES), lambda p, i: (jnp.where(p == nj, i, 0), 0))]
        out_shape += [jax.ShapeDtypeStruct((m, n), BF16), jax.ShapeDtypeStruct((m, LANES), F32)]
        scratch.append(pltpu.VMEM((m, LANES), F32))
    out_bytes = jnp.dtype(out_dtype).itemsize
    vmem = 2 * (tm * tk * 2 + ck * tn * 4 + tm * tn * out_bytes) + 2 * tk * tn * 2
    vmem += 2 * tm * tn * 4 if has_res else 0
    vmem += 2 * tm * LANES * 4 if has_scale else 0
    vmem += 6 * tm * HEAD_DIM * 4 if has_rope else 0
    vmem += 2 * (tm * tn * 2 + tm * LANES * 4) + m * LANES * 4 if emit_norm else 0
    vmem += tm * tn * 4
    outs = pl.pallas_call(
        functools.partial(_wsmm_kernel, tm=tm, ck=ck, nc=nc, n_sub=n_sub, relu2=relu2, has_res=has_res, has_scale=has_scale,
                          has_rope=has_rope, emit_norm=emit_norm, d_norm=float(x.shape[1]) if has_scale else None),
        grid=(nj + 1, ni),
        in_specs=in_specs,
        out_specs=out_specs,
        out_shape=out_shape,
        scratch_shapes=scratch,
        compiler_params=pltpu.CompilerParams(
            dimension_semantics=("arbitrary", "arbitrary"),
            vmem_limit_bytes=_vmem_limit(vmem)),
        name=name,
    )(*args)
    return outs if emit_norm else outs[0]


def _sigmoid(x):
    return 1.0 / (1.0 + jnp.exp(-x))


def _layernorm_silu(c, g, b):
    mu = jnp.mean(c, axis=-1, keepdims=True)
    xc = c - mu
    var = jnp.mean(xc * xc, axis=-1, keepdims=True)
    y = xc * lax.rsqrt(var + EPS) * g + b
    return y * _sigmoid(y)


def _rope_tables(pos):
    half = ROT_DIM // 2
    inv_freq = jnp.exp(-math.log(ROPE_THETA) * 2.0 * jnp.arange(half, dtype=F32) / ROT_DIM)
    ang = pos.astype(F32)[:, None] * inv_freq[None, :]
    cos, sin = jnp.cos(ang), jnp.sin(ang)
    s = pos.shape[0]
    ones = jnp.ones((s, HEAD_DIM - ROT_DIM), F32)
    zeros = jnp.zeros((s, HEAD_DIM - ROT_DIM), F32)
    zh = jnp.zeros((s, half), F32)
    cos_t = jnp.concatenate([cos, cos, ones], axis=-1)
    sin_lo = jnp.concatenate([-sin, zh, zeros], axis=-1)
    sin_hi = jnp.concatenate([zh, sin, zeros], axis=-1)
    return cos_t, sin_lo, sin_hi


def _pool_group(zbuf, base, rows, g, w, pos0, wpool_ref, pscale_ref):
    ls = slice(g * POOL_GROUP, (g + 1) * POOL_GROUP)
    cur = zbuf[base:base + rows, ls]
    s = cur
    for i in range(1, w):
        s = s + zbuf[base - i:base - i + rows, ls]
    pos = pos0 + lax.broadcasted_iota(jnp.int32, (rows, 1), 0)
    cnt = jnp.minimum(pos + 1, w).astype(F32)
    p = (s / cnt - cur).astype(BF16)
    y = jnp.dot(p, wpool_ref[g].astype(BF16), preferred_element_type=F32)
    return y * pscale_ref[:, ls]


def _convpool_kernel(proj_ref, chist_ref, phist_ref, wconv_ref, bconv_ref, lng_ref, lnb_ref,
                     wpool_ref, pscale_ref, mix_in_ref, mix_ref, cnew_ref, pnew_ref, ubuf, ush, zbuf, cbuf,
                     *, tile, n_tiles):
    del mix_in_ref
    t = pl.program_id(1)

    @pl.when(t == 0)
    def _():
        ubuf[0:CONV_HALO - CONV_BUF, :] = jnp.zeros((CONV_HALO - CONV_BUF, D_CONV), F32)
        ubuf[CONV_HALO - CONV_BUF:CONV_HALO, :] = chist_ref[0]
        zbuf[0:POOL_HALO - POOL_BUF, :] = jnp.zeros((POOL_HALO - POOL_BUF, D_POOL), F32)
        zbuf[POOL_HALO - POOL_BUF:POOL_HALO, :] = phist_ref[0]

    a = proj_ref[:, OFF_A:OFF_A + D_CONV]
    b = proj_ref[:, OFF_B:OFF_B + D_CONV]
    ubuf[CONV_HALO:CONV_HALO + tile, :] = a * _sigmoid(b)
    zbuf[POOL_HALO:POOL_HALO + tile, :] = proj_ref[:, OFF_Z:OFF_Z + D_POOL]

    span = tile + CONV_HALO - SUBLANES
    for s in range(1, SUBLANES):
        ush[s - 1, 0:span, :] = ubuf[s:s + span, :]
    rc, lc = 64, 256
    first = CONV_HALO - CONV_BUF
    for r0 in range(0, tile, rc):
        for c in range(D_CONV // lc):
            ls = slice(c * lc, (c + 1) * lc)
            acc = None
            for k in range(CONV_WIDTH):
                q, s = divmod(first + k, SUBLANES)
                rows = slice(r0 + q * SUBLANES, r0 + q * SUBLANES + rc)
                src = ubuf[rows, ls] if s == 0 else ush[s - 1, rows, ls]
                term = src * wconv_ref[k:k + 1, ls]
                acc = term if acc is None else acc + term
            cbuf[r0:r0 + rc, ls] = acc + bconv_ref[:, ls]
    c = _layernorm_silu(cbuf[...], lng_ref[...], lnb_ref[...])
    mix_ref[:, 0:D_CONV] = c.astype(mix_ref.dtype)

    for g, w in enumerate(POOL_WINDOWS):
        y = _pool_group(zbuf, POOL_HALO, tile, g, w, t * tile, wpool_ref, pscale_ref)
        mix_ref[:, D_CONV + g * POOL_GROUP:D_CONV + (g + 1) * POOL_GROUP] = y.astype(mix_ref.dtype)

    @pl.when(t == n_tiles - 1)
    def _():
        cnew_ref[0] = ubuf[CONV_HALO + tile - CONV_BUF:CONV_HALO + tile, :]
        pnew_ref[0] = zbuf[POOL_HALO + tile - POOL_BUF:POOL_HALO + tile, :]

    ubuf[0:CONV_HALO, :] = ubuf[tile:tile + CONV_HALO, :]
    zbuf[0:POOL_HALO, :] = zbuf[tile:tile + POOL_HALO, :]


def _convpool_prompt(proj, mix, chist, phist, wconv, bconv, lng, lnb, wpool, pscale, *, batch, seq):
    tile = _divisor_tile(seq, 256, 2 * CONV_HALO)
    n_tiles = seq // tile
    row = lambda b, t: (b * n_tiles + t, 0)
    const2 = lambda b, t: (0, 0)
    per_batch = lambda b, t: (b, 0, 0)
    width = OFF_Q
    return pl.pallas_call(
        functools.partial(_convpool_kernel, tile=tile, n_tiles=n_tiles),
        grid=(batch, n_tiles),
        in_specs=[pl.BlockSpec((tile, width), row),
                  pl.BlockSpec((1, CONV_BUF, D_CONV), per_batch),
                  pl.BlockSpec((1, POOL_BUF, D_POOL), per_batch),
                  pl.BlockSpec((CONV_WIDTH, D_CONV), const2),
                  pl.BlockSpec((1, D_CONV), const2),
                  pl.BlockSpec((1, D_CONV), const2),
                  pl.BlockSpec((1, D_CONV), const2),
                  pl.BlockSpec((len(POOL_WINDOWS), POOL_GROUP, POOL_GROUP), lambda b, t: (0, 0, 0)),
                  pl.BlockSpec((1, D_POOL), const2),
                  pl.BlockSpec(memory_space=pl.ANY)],
        out_specs=[pl.BlockSpec((tile, D_CONV + D_POOL), row),
                   pl.BlockSpec((1, CONV_BUF, D_CONV), per_batch),
                   pl.BlockSpec((1, POOL_BUF, D_POOL), per_batch)],
        out_shape=[jax.ShapeDtypeStruct(mix.shape, mix.dtype),
                   jax.ShapeDtypeStruct((batch, CONV_BUF, D_CONV), F32),
                   jax.ShapeDtypeStruct((batch, POOL_BUF, D_POOL), F32)],
        scratch_shapes=[pltpu.VMEM((CONV_HALO + tile, D_CONV), F32),
                        pltpu.VMEM((SUBLANES - 1, CONV_HALO + tile, D_CONV), F32),
                        pltpu.VMEM((POOL_HALO + tile, D_POOL), F32),
                        pltpu.VMEM((tile, D_CONV), F32)],
        input_output_aliases={9: 0},
        compiler_params=pltpu.CompilerParams(dimension_semantics=("arbitrary", "arbitrary")),
        name="convpool_prompt",
    )(proj, chist, phist, wconv, bconv.reshape(1, -1), lng.reshape(1, -1), lnb.reshape(1, -1),
      wpool, pscale.reshape(1, -1), mix)


BLOCK_UNROLL = 4


def _softmax_sink_pv(s, sink_col, vv):
    m = jnp.maximum(jnp.max(s, axis=-1, keepdims=True), sink_col)
    e = jnp.exp(s - m)
    den = jnp.sum(e, axis=-1, keepdims=True) + jnp.exp(sink_col - m)
    return jnp.dot(e.astype(BF16), vv, preferred_element_type=F32) / den


def _attn_prompt_kernel(sink_ref, q_ref, k_ref, v_ref, mix_in_ref, o_ref, knew_ref, vnew_ref, kbuf, vbuf, bias,
                        *, seq):
    del mix_in_ref
    kh = pl.program_id(1)
    blk = ATTN_BLOCK
    n_blocks = seq // blk
    rows_all = Q_PER_KV * blk

    kbuf[0:blk, :] = jnp.zeros((blk, HEAD_DIM), BF16)
    vbuf[0:blk, :] = jnp.zeros((blk, HEAD_DIM), BF16)

    row = lax.broadcasted_iota(jnp.int32, (rows_all, 1), 0)
    qi = jnp.bitwise_and(row, blk - 1)
    kj = lax.broadcasted_iota(jnp.int32, (1, 2 * blk), 1)
    band = (kj >= qi) & (kj <= qi + WINDOW)
    bias[1] = jnp.where(band, 0.0, -jnp.inf)
    bias[0] = jnp.where(band & (kj >= blk), 0.0, -jnp.inf)
    sink_col = jnp.zeros((rows_all, 1), F32)
    for g in range(Q_PER_KV):
        sink_col = jnp.where((row >= g * blk) & (row < (g + 1) * blk), sink_ref[kh * Q_PER_KV + g], sink_col)
    scale = HEAD_DIM ** -0.5

    last = slice(seq - KV_BUF, seq)
    knew_ref[0] = k_ref[last, :]
    vnew_ref[0] = v_ref[last, :]
    kbuf[blk:blk + seq, :] = k_ref[...].astype(BF16)
    vbuf[blk:blk + seq, :] = v_ref[...].astype(BF16)

    def block(n, carry):
        r0 = pl.multiple_of(n * blk, blk)
        rows = pl.ds(r0, blk)
        q4 = jnp.concatenate(
            [q_ref[rows, g * HEAD_DIM:(g + 1) * HEAD_DIM].astype(BF16) for g in range(Q_PER_KV)], axis=0)
        kk = kbuf[pl.ds(r0, 2 * blk), :]
        vv = vbuf[pl.ds(r0, 2 * blk), :]
        s = lax.dot_general(q4, kk, (((1,), (1,)), ((), ())), preferred_element_type=F32) * scale
        s = s + bias[jnp.minimum(n, 1)]
        o = _softmax_sink_pv(s, sink_col, vv)
        for g in range(Q_PER_KV):
            o_ref[rows, g * HEAD_DIM:(g + 1) * HEAD_DIM] = o[g * blk:(g + 1) * blk, :].astype(o_ref.dtype)
        return carry

    lax.fori_loop(0, n_blocks, block, 0, unroll=BLOCK_UNROLL)


def _attn_prompt(proj, mix, sinks, *, batch, seq):
    m = proj.shape[0]
    gw = Q_PER_KV * HEAD_DIM
    return pl.pallas_call(
        functools.partial(_attn_prompt_kernel, seq=seq),
        grid=(batch, N_KV_HEADS),
        in_specs=[pl.BlockSpec(memory_space=pltpu.SMEM),
                  pl.BlockSpec((seq, gw), lambda b, h: (b, OFF_Q // gw + h)),
                  pl.BlockSpec((seq, HEAD_DIM), lambda b, h: (b, OFF_K // HEAD_DIM + h)),
                  pl.BlockSpec((seq, HEAD_DIM), lambda b, h: (b, OFF_V // HEAD_DIM + h)),
                  pl.BlockSpec(memory_space=pl.ANY)],
        out_specs=[pl.BlockSpec((seq, gw), lambda b, h: (b, (D_CONV + D_POOL) // gw + h)),
                   pl.BlockSpec((1, KV_BUF, HEAD_DIM), lambda b, h: (b, 0, h)),
                   pl.BlockSpec((1, KV_BUF, HEAD_DIM), lambda b, h: (b, 0, h))],
        out_shape=[jax.ShapeDtypeStruct((m, D_MIX), BF16),
                   jax.ShapeDtypeStruct((batch, KV_BUF, D_KV), F32),
                   jax.ShapeDtypeStruct((batch, KV_BUF, D_KV), F32)],
        scratch_shapes=[pltpu.VMEM((seq + ATTN_BLOCK, HEAD_DIM), BF16),
                        pltpu.VMEM((seq + ATTN_BLOCK, HEAD_DIM), BF16),
                        pltpu.VMEM((2, Q_PER_KV * ATTN_BLOCK, 2 * ATTN_BLOCK), F32)],
        input_output_aliases={4: 0},
        compiler_params=pltpu.CompilerParams(dimension_semantics=("arbitrary", "arbitrary")),
        name="attn_prompt",
    )(sinks, proj, proj, proj, mix)


KEYS_PAD = 256


def _sample_one(bb, row0, sink_ref, proj_ref, chist_ref, phist_ref, ck_ref, cv_ref, wconv_ref, bconv_ref,
                lng_ref, lnb_ref, wpool_ref, pscale_ref,
                cnew_ref, pnew_ref, knew_ref, vnew_ref, fbuf, pbuf, kkbuf, vvbuf, qbuf, obuf, mstep, ts):
    out_rows = slice(row0, row0 + ts)
    a = proj_ref[bb, :, OFF_A:OFF_A + D_CONV]
    b = proj_ref[bb, :, OFF_B:OFF_B + D_CONV]
    fbuf[0:CONV_BUF, :] = chist_ref[bb]
    fbuf[CONV_BUF:CONV_BUF + ts, :] = a * _sigmoid(b)
    acc = fbuf[0:ts, :] * wconv_ref[0:1, :]
    for k in range(1, CONV_WIDTH):
        acc = acc + fbuf[k:k + ts, :] * wconv_ref[k:k + 1, :]
    mstep[out_rows, 0:D_CONV] = _layernorm_silu(acc + bconv_ref[...], lng_ref[...], lnb_ref[...])
    cnew_ref[bb] = fbuf[ts:ts + CONV_BUF, :]

    pbuf[0:POOL_BUF, :] = phist_ref[bb]
    pbuf[POOL_BUF:POOL_BUF + ts, :] = proj_ref[bb, :, OFF_Z:OFF_Z + D_POOL]
    for g, w in enumerate(POOL_WINDOWS):
        y = _pool_group(pbuf, POOL_BUF, ts, g, w, PAST_LEN, wpool_ref, pscale_ref)
        mstep[out_rows, D_CONV + g * POOL_GROUP:D_CONV + (g + 1) * POOL_GROUP] = y
    pnew_ref[bb] = pbuf[ts:ts + POOL_BUF, :]

    for h in range(N_KV_HEADS):
        hs = slice(h * HEAD_DIM, (h + 1) * HEAD_DIM)
        kkbuf[0:KV_BUF, hs] = ck_ref[bb, :, h, :]
        vvbuf[0:KV_BUF, hs] = cv_ref[bb, :, h, :]
    kkbuf[KV_BUF:KV_BUF + ts, :] = proj_ref[bb, :, OFF_K:OFF_K + D_KV]
    vvbuf[KV_BUF:KV_BUF + ts, :] = proj_ref[bb, :, OFF_V:OFF_V + D_KV]
    pad = KEYS_PAD - KV_BUF - ts
    kkbuf[KV_BUF + ts:KEYS_PAD, :] = jnp.zeros((pad, D_KV), F32)
    vvbuf[KV_BUF + ts:KEYS_PAD, :] = jnp.zeros((pad, D_KV), F32)
    for h in range(N_KV_HEADS):
        hs = slice(h * HEAD_DIM, (h + 1) * HEAD_DIM)
        knew_ref[bb, :, h, :] = kkbuf[ts:ts + KV_BUF, hs]
        vnew_ref[bb, :, h, :] = vvbuf[ts:ts + KV_BUF, hs]

    rows_all = Q_PER_KV * ts
    row = lax.broadcasted_iota(jnp.int32, (rows_all, 1), 0)
    tq = row % ts
    kj = lax.broadcasted_iota(jnp.int32, (1, KEYS_PAD), 1)
    valid = (kj >= tq + (KV_BUF - WINDOW)) & (kj <= tq + KV_BUF) & (kj < KV_BUF + ts)
    scale = HEAD_DIM ** -0.5
    for kh in range(N_KV_HEADS):
        hs = slice(kh * HEAD_DIM, (kh + 1) * HEAD_DIM)
        sink_col = jnp.zeros((rows_all, 1), F32)
        for g in range(Q_PER_KV):
            h = kh * Q_PER_KV + g
            qbuf[g * ts:(g + 1) * ts, :] = proj_ref[bb, :, OFF_Q + h * HEAD_DIM:OFF_Q + (h + 1) * HEAD_DIM]
            sink_col = jnp.where((row >= g * ts) & (row < (g + 1) * ts), sink_ref[h], sink_col)
        q = qbuf[...].astype(BF16)
        kk = kkbuf[:, hs].astype(BF16)
        vv = vvbuf[:, hs].astype(BF16)
        s = lax.dot_general(q, kk, (((1,), (1,)), ((), ())), preferred_element_type=F32) * scale
        s = jnp.where(valid, s, -jnp.inf)
        obuf[...] = _softmax_sink_pv(s, sink_col, vv)
        for g in range(Q_PER_KV):
            h = kh * Q_PER_KV + g
            col = D_CONV + D_POOL + h * HEAD_DIM
            mstep[out_rows, col:col + HEAD_DIM] = obuf[g * ts:(g + 1) * ts, :]


def _sample_kernel(*refs, steps, seqs, n_steps, n_alias):
    (sink_ref, proj_ref, chist_ref, phist_ref, ck_ref, cv_ref, wconv_ref, bconv_ref,
     lng_ref, lnb_ref, wpool_ref, pscale_ref) = refs[:12]
    (mix_ref, cnew_ref, pnew_ref, knew_ref, vnew_ref, fbuf, pbuf, kkbuf, vvbuf, qbuf, obuf,
     mstep, mixacc) = refs[12 + n_alias:]
    step = pl.program_id(0)
    for bb in range(seqs):
        _sample_one(bb, bb * steps, sink_ref, proj_ref, chist_ref, phist_ref, ck_ref, cv_ref, wconv_ref,
                    bconv_ref, lng_ref, lnb_ref, wpool_ref, pscale_ref,
                    cnew_ref, pnew_ref, knew_ref, vnew_ref, fbuf.at[bb], pbuf.at[bb], kkbuf.at[bb], vvbuf.at[bb],
                    qbuf.at[bb], obuf.at[bb], mstep, steps)
    rows = seqs * steps
    mixacc[pl.ds(pl.multiple_of(step * rows, rows), rows), :] = mstep[...]

    @pl.when(step == n_steps - 1)
    def _():
        mix_ref[...] = mixacc[...].astype(mix_ref.dtype)


def _sample_mixers(proj, mix, chist, phist, ck, cv, sinks, wconv, bconv, lng, lnb, wpool, pscale,
                   *, row0, layer, states):
    nb, ts, _ = proj.shape
    depth = ck.shape[0]
    assert SUBLANES % ts == 0
    seqs = SUBLANES // ts
    assert nb % seqs == 0 and row0 % (nb * ts) == 0 and (nb * ts) % SUBLANES_BF16 == 0
    n_steps = nb // seqs
    per_step = lambda s: (s, 0, 0)
    const2 = lambda s: (0, 0)
    vec = lambda n: pl.BlockSpec((1, n), const2)
    stacked = lambda s: (layer, s, 0, 0)
    stacked_kv = lambda s: (layer, s, 0, 0, 0)
    kv_block = (None, seqs, KV_BUF, N_KV_HEADS, HEAD_DIM)
    aliased = [mix] + list(states)
    n_fixed = 12
    return pl.pallas_call(
        functools.partial(_sample_kernel, steps=ts, seqs=seqs, n_steps=n_steps, n_alias=len(aliased)),
        grid=(n_steps,),
        in_specs=[pl.BlockSpec(memory_space=pltpu.SMEM),
                  pl.BlockSpec((seqs, ts, D_IN), per_step),
                  pl.BlockSpec((None, seqs, CONV_BUF, D_CONV), stacked),
                  pl.BlockSpec((None, seqs, POOL_BUF, D_POOL), stacked),
                  pl.BlockSpec(kv_block, stacked_kv),
                  pl.BlockSpec(kv_block, stacked_kv),
                  pl.BlockSpec((CONV_WIDTH, D_CONV), const2),
                  vec(D_CONV), vec(D_CONV), vec(D_CONV),
                  pl.BlockSpec((len(POOL_WINDOWS), POOL_GROUP, POOL_GROUP), lambda s: (0, 0, 0)),
                  vec(D_POOL)] + [pl.BlockSpec(memory_space=pl.ANY)] * len(aliased),
        out_specs=[pl.BlockSpec((nb * ts, D_MIX), lambda s: (row0 // (nb * ts), 0)),
                   pl.BlockSpec((None, seqs, CONV_BUF, D_CONV), stacked),
                   pl.BlockSpec((None, seqs, POOL_BUF, D_POOL), stacked),
                   pl.BlockSpec(kv_block, stacked_kv),
                   pl.BlockSpec(kv_block, stacked_kv)],
        out_shape=[jax.ShapeDtypeStruct(mix.shape, mix.dtype),
                   jax.ShapeDtypeStruct((depth, nb, CONV_BUF, D_CONV), F32),
                   jax.ShapeDtypeStruct((depth, nb, POOL_BUF, D_POOL), F32),
                   jax.ShapeDtypeStruct((depth, nb, KV_BUF, N_KV_HEADS, HEAD_DIM), F32),
                   jax.ShapeDtypeStruct((depth, nb, KV_BUF, N_KV_HEADS, HEAD_DIM), F32)],
        scratch_shapes=[pltpu.VMEM((seqs, CONV_BUF + 2 * ts + 2, D_CONV), F32),
                        pltpu.VMEM((seqs, POOL_BUF + 2 * ts + 1, D_POOL), F32),
                        pltpu.VMEM((seqs, KEYS_PAD, D_KV), F32),
                        pltpu.VMEM((seqs, KEYS_PAD, D_KV), F32),
                        pltpu.VMEM((seqs, Q_PER_KV * ts, HEAD_DIM), F32),
                        pltpu.VMEM((seqs, Q_PER_KV * ts, HEAD_DIM), F32),
                        pltpu.VMEM((seqs * ts, D_MIX), F32),
                        pltpu.VMEM((nb * ts, D_MIX), F32)],
        input_output_aliases={n_fixed + a: a for a in range(len(aliased))},
        compiler_params=pltpu.CompilerParams(dimension_semantics=("arbitrary",)),
        name="sample_mixers",
    )(sinks, proj, chist, phist, ck, cv, wconv, bconv.reshape(1, -1), lng.reshape(1, -1), lnb.reshape(1, -1),
      wpool, pscale.reshape(1, -1), *aliased)


def kernel(x_prompt, x_sample, state_conv, state_pool, cache_k, cache_v, norm_mix, w_in, w_conv, b_conv,
           ln_conv_g, ln_conv_b, w_pool, pool_scale, q_norm, k_norm, attn_sinks, w_out, norm_ffn, w_up, w_down):
    bp, sp, d_model = x_prompt.shape
    bs, ts, _ = x_sample.shape
    depth = w_in.shape[0]
    assert w_in.shape[2] == D_IN and sp % ATTN_BLOCK == 0 and KV_BUF == WINDOW
    mp, ms = bp * sp, bs * ts

    tab_p = _rope_tables(jnp.arange(sp, dtype=jnp.int32))
    tab_s = _rope_tables(PAST_LEN + jnp.arange(ts, dtype=jnp.int32))
    tables = tuple(jnp.concatenate([jnp.tile(tp, (bp, 1)), jnp.tile(tsm, (bs, 1))], axis=0)
                   for tp, tsm in zip(tab_p, tab_s))
    head_mask = jnp.zeros((D_IN,), F32).at[OFF_Q:OFF_V].set(1.0)
    zero_conv = jnp.zeros((bp, CONV_BUF, D_CONV), F32)
    zero_pool = jnp.zeros((bp, POOL_BUF, D_POOL), F32)
    n_chunks = w_up.shape[2] // d_model

    mix = jnp.zeros((mp + ms, D_MIX), BF16)
    states_s = [jnp.zeros(a.shape, F32) for a in (state_conv, state_pool, cache_k, cache_v)]

    outs = [[] for _ in range(4)]
    x, xg, ssq = _merge_prenorm(x_prompt.reshape(mp, d_model), x_sample.reshape(ms, d_model), norm_mix[0])
    for l in range(depth):
        mixer_w = (w_conv[l], b_conv[l], ln_conv_g[l], ln_conv_b[l], w_pool[l], pool_scale[l])
        head_gain = jnp.concatenate([jnp.ones((OFF_Q,), F32), jnp.tile(q_norm[l], N_HEADS),
                                     jnp.tile(k_norm[l], N_KV_HEADS), jnp.ones((D_KV,), F32)])

        proj = _wsmm(xg, w_in, l, ssq=ssq, rope=(head_gain, head_mask) + tables, tm_target=ROPE_TM, n_sub=ROPE_SUB,
                     name="in_proj")
        mix, conv_p, pool_p = _convpool_prompt(proj, mix, zero_conv, zero_pool, *mixer_w, batch=bp, seq=sp)
        mix, k_p, v_p = _attn_prompt(proj, mix, attn_sinks[l], batch=bp, seq=sp)
        mix, *states_s = _sample_mixers(
            proj[mp:].reshape(bs, ts, D_IN), mix, state_conv, state_pool, cache_k, cache_v,
            attn_sinks[l], *mixer_w, row0=mp, layer=l, states=states_s)
        x, xg, ssq = _wsmm(mix, w_out, l, res=x, norm_gain=norm_ffn[l], tm_target=NORM_TM, name="out_proj")

        up = _wsmm(xg, w_up, l, ssq=ssq, relu2=True, out_dtype=BF16, name="ffn_up")
        for c in range(n_chunks):
            if c == n_chunks - 1 and l + 1 < depth:
                x, xg, ssq = _wsmm(up, w_down, l, k_chunk=c, res=x, norm_gain=norm_mix[l + 1], tm_target=NORM_TM,
                                   name="ffn_down")
            else:
                x = _wsmm(up, w_down, l, k_chunk=c, res=x, tm_target=RES_TM, name="ffn_down")

        for lst, val in zip(outs, (conv_p, pool_p, k_p, v_p)):
            lst.append(val)

    conv_p, pool_p, k_p, v_p = [jnp.stack(o) for o in outs]
    conv_s, pool_s, k_s, v_s = states_s
    kv_shape_p = (depth, bp, KV_BUF, N_KV_HEADS, HEAD_DIM)
    return (x[:mp].reshape(bp, sp, d_model), x[mp:].reshape(bs, ts, d_model),
            conv_p, pool_p, k_p.reshape(kv_shape_p), v_p.reshape(kv_shape_p), conv_s, pool_s, k_s, v_s)
```

```python
import functools
import math

import jax
import jax.numpy as jnp
from jax import lax
from jax.experimental import pallas as pl
from jax.experimental.pallas import tpu as pltpu

F32 = jnp.float32
BF16 = jnp.bfloat16

HEAD_DIM = 128
N_KV_HEADS = 4
Q_PER_KV = 4
N_HEADS = N_KV_HEADS * Q_PER_KV
D_ATTN = N_HEADS * HEAD_DIM
D_KV = N_KV_HEADS * HEAD_DIM
D_CONV = 1024
D_POOL = 1024
D_MIX = D_CONV + D_POOL + D_ATTN
ROT_DIM = HEAD_DIM // 4
ROPE_THETA = 500000.0
WINDOW = 128
ATTN_BLOCK = 128
KV_BUF = 128
CONV_WIDTH = 31
CONV_BUF = CONV_WIDTH - 1
POOL_WINDOWS = (2, 4, 8, 16)
POOL_GROUP = D_POOL // len(POOL_WINDOWS)
POOL_BUF = max(POOL_WINDOWS) - 1
PAST_LEN = 8192
EPS = 1e-6

OFF_A = 0
OFF_B = D_CONV
OFF_Z = 2 * D_CONV
OFF_Q = 2 * D_CONV + D_POOL
OFF_K = OFF_Q + D_ATTN
OFF_V = OFF_K + D_KV
D_IN = OFF_V + D_KV

V7X_VMEM_BYTES = 64 * 1024 * 1024
LANES = 128
SUBLANES = 8
SUBLANES_BF16 = 16

ROPE_TM = 1040
ROPE_SUB = 5
RES_TM = 832
NORM_TM = 640
CONV_HALO = 32
POOL_HALO = 24
POOL_SPAN = 16
assert POOL_WINDOWS[0] == 2 and POOL_SPAN > POOL_BUF and POOL_HALO - POOL_SPAN >= max(POOL_WINDOWS) // 2


def _divisor_tile(n, target, multiple):
    best = None
    for d in range(multiple, min(n, target) + 1, multiple):
        if n % d == 0:
            best = d
    return n if best is None else best


def _vmem_limit(nbytes):
    return int(min(V7X_VMEM_BYTES - 2 * 1024 * 1024, nbytes + 6 * 1024 * 1024))


def _fold_lanes(sq):
    out = sq[:, 0:LANES]
    for c in range(1, sq.shape[1] // LANES):
        out = out + sq[:, c * LANES:(c + 1) * LANES]
    return out


def _row_rstd(ssq, d_norm):
    return lax.rsqrt(jnp.sum(ssq, axis=-1, keepdims=True) / d_norm + EPS)


def _merge_prenorm_kernel(xp_ref, xs_ref, g_ref, x_ref, xg_ref, ssq_ref, *, n_prompt_tiles):
    x = jnp.where(pl.program_id(0) < n_prompt_tiles, xp_ref[...], xs_ref[...])
    x_ref[...] = x
    xg_ref[...] = (x * g_ref[...]).astype(xg_ref.dtype)
    ssq_ref[...] = _fold_lanes(x * x)


def _merge_prenorm(x_prompt, x_sample, g):
    (mp, d), ms = x_prompt.shape, x_sample.shape[0]
    assert mp % ms == 0 and ms % SUBLANES_BF16 == 0
    npt = mp // ms
    m = mp + ms
    row = lambda i: (i, 0)
    return pl.pallas_call(
        functools.partial(_merge_prenorm_kernel, n_prompt_tiles=npt),
        grid=(npt + 1,),
        in_specs=[pl.BlockSpec((ms, d), lambda i: (jnp.minimum(i, npt - 1), 0)),
                  pl.BlockSpec((ms, d), lambda i: (0, 0)),
                  pl.BlockSpec((1, d), lambda i: (0, 0))],
        out_specs=[pl.BlockSpec((ms, d), row), pl.BlockSpec((ms, d), row), pl.BlockSpec((ms, LANES), row)],
        out_shape=[jax.ShapeDtypeStruct((m, d), F32),
                   jax.ShapeDtypeStruct((m, d), BF16),
                   jax.ShapeDtypeStruct((m, LANES), F32)],
        compiler_params=pltpu.CompilerParams(dimension_semantics=("arbitrary",)),
        name="merge_prenorm",
    )(x_prompt, x_sample, g.reshape(1, d))


def _head_norm_rope(acc, gain_ref, mask_ref, cos, sin_lo, sin_hi):
    half = ROT_DIM // 2
    cols = []
    for c in range(acc.shape[1] // HEAD_DIM):
        cs = slice(c * HEAD_DIM, (c + 1) * HEAD_DIM)
        a = acc[:, cs]
        y = a * lax.rsqrt(jnp.mean(a * a, axis=-1, keepdims=True) + EPS) * gain_ref[:, cs]
        up = pltpu.roll(y, HEAD_DIM - half, 1)
        down = pltpu.roll(y, half, 1)
        cols.append(jnp.where(mask_ref[:, cs] > 0.0, y * cos + up * sin_lo + down * sin_hi, a))
    return jnp.concatenate(cols, axis=1)


def _wsmm_kernel(*refs, tm, ck, nc, n_sub, tail, relu2, has_res, has_scale, has_rope, emit_norm, d_norm):
    it = iter(refs)
    x_ref, w_ref = next(it), next(it)
    res_ref = next(it) if has_res else None
    ssq_in_ref = next(it) if has_scale else None
    rope_refs = [next(it) for _ in range(5)] if has_rope else None
    g_ref = next(it) if emit_norm else None
    o_ref = next(it)
    tail_ref = next(it) if tail else None
    xg_ref, ssq_out_ref = (next(it), next(it)) if emit_norm else (None, None)
    wbf = next(it)
    ssq_acc = next(it) if emit_norm else None
    p = pl.program_id(0)
    i = pl.program_id(1)
    rows = pl.ds(pl.multiple_of(i * tm, tm), tm)

    def cast_chunk():
        wrows = pl.ds(pl.multiple_of(jnp.minimum(i, nc - 1) * ck, ck), ck)
        wbf[p % 2, wrows, :] = w_ref[...].astype(BF16)

    @pl.when(p == 0)
    def _():
        cast_chunk()
        if emit_norm:
            ssq_acc[rows, :] = jnp.zeros((tm, LANES), F32)

    def sub_tile(r):
        acc = jnp.dot(x_ref[r, :], wbf[(p + 1) % 2], preferred_element_type=F32)
        if has_scale:
            acc = acc * _row_rstd(ssq_in_ref[r, :], d_norm)
        if has_rope:
            gain_ref, mask_ref, cos_ref, slo_ref, shi_ref = rope_refs
            acc = _head_norm_rope(acc, gain_ref, mask_ref, cos_ref[r, :], slo_ref[r, :], shi_ref[r, :])
        if relu2:
            acc = jnp.square(jnp.maximum(acc, 0.0))
        if has_res:
            acc = res_ref[r, :] + acc
        o_ref[r, :] = acc.astype(o_ref.dtype)
        if tail:
            tail_ref[...] = acc[tail[0]:tail[0] + tail[1], :].astype(tail_ref.dtype)
        if emit_norm:
            xg_ref[r, :] = (acc * g_ref[...]).astype(xg_ref.dtype)
            acc_rows = pl.ds(pl.multiple_of(i * tm + r.start, SUBLANES_BF16), r.stop - r.start)
            total = ssq_acc[acc_rows, :] + _fold_lanes(acc * acc)
            ssq_acc[acc_rows, :] = total
            ssq_out_ref[r, :] = total

    @pl.when(p > 0)
    def _():
        sub = tm // n_sub
        for h in range(n_sub):
            sub_tile(slice(h * sub, (h + 1) * sub))
        cast_chunk()


def _wsmm(x, w, layer, *, k_chunk=0, res=None, ssq=None, rope=None, norm_gain=None, relu2=False, out_dtype=F32,
          tm_target=1040, n_sub=1, split_rows=None, tn=1024, tk=4096, name="wsmm"):
    m = x.shape[0]
    n = w.shape[2]
    assert x.shape[1] % tk == 0 and w.shape[1] == x.shape[1] and n % tn == 0
    tm = _divisor_tile(m, tm_target, SUBLANES_BF16 * n_sub)
    ni, nj = m // tm, n // tn
    nc = max(c for c in range(1, ni + 1) if tk % c == 0 and (tk // c) % SUBLANES_BF16 == 0)
    ck = tk // nc
    has_res, has_scale, emit_norm = res is not None, ssq is not None, norm_gain is not None
    has_rope = rope is not None

    def x_map(p, i):
        return jnp.where(p == 0, 0, i), k_chunk

    def w_map(p, i):
        return layer, k_chunk * nc + jnp.where(p < nj, jnp.minimum(i, nc - 1), nc - 1), jnp.minimum(p, nj - 1)

    def o_map(p, i):
        return jnp.where(p == 0, 0, i), jnp.maximum(p - 1, 0)

    def row_map(p, i):
        return jnp.where(p == 0, 0, i), 0

    in_specs = [pl.BlockSpec((tm, tk), x_map), pl.BlockSpec((None, ck, tn), w_map)]
    args = [x, w]
    if has_res:
        in_specs.append(pl.BlockSpec((tm, tn), o_map))
        args.append(res)
    if has_scale:
        in_specs.append(pl.BlockSpec((tm, LANES), row_map))
        args.append(ssq)
    col_vec = pl.BlockSpec((1, tn), lambda p, i: (0, jnp.maximum(p - 1, 0)))
    if has_rope:
        gains, mask, cos_t, sin_lo, sin_hi = rope
        in_specs += [col_vec, col_vec] + [pl.BlockSpec((tm, HEAD_DIM), row_map)] * 3
        args += [gains.reshape(1, n), mask.reshape(1, n), cos_t, sin_lo, sin_hi]
    out_specs = [pl.BlockSpec((tm, tn), o_map)]
    out_shape = [jax.ShapeDtypeStruct((m, n), out_dtype)]
    tail = None
    if split_rows is not None:
        n_tail = m - split_rows
        tail = (split_rows - (ni - 1) * tm, n_tail)
        assert n_sub == 1 and 0 < tail[0] and tail[0] % SUBLANES == 0 and n_tail % SUBLANES == 0
        out_shape = [jax.ShapeDtypeStruct((split_rows, n), out_dtype), jax.ShapeDtypeStruct((n_tail, n), out_dtype)]
        out_specs.append(pl.BlockSpec((n_tail, tn), lambda p, i: (0, jnp.maximum(p - 1, 0))))
    scratch = [pltpu.VMEM((2, tk, tn), BF16)]
    if emit_norm:
        in_specs.append(col_vec)
        args.append(norm_gain.reshape(1, n))
        out_specs += [pl.BlockSpec((tm, tn), o_map),
                      pl.BlockSpec((tm, LANES), lambda p, i: (jnp.where(p == nj, i, 0), 0))]
        out_shape += [jax.ShapeDtypeStruct((m, n), BF16), jax.ShapeDtypeStruct((m, LANES), F32)]
        scratch.append(pltpu.VMEM((m, LANES), F32))
    out_bytes = jnp.dtype(out_dtype).itemsize
    vmem = 2 * (tm * tk * 2 + ck * tn * 4 + tm * tn * out_bytes) + 2 * tk * tn * 2
    vmem += 2 * tm * tn * 4 if has_res else 0
    vmem += 2 * tm * LANES * 4 if has_scale else 0
    vmem += 6 * tm * HEAD_DIM * 4 if has_rope else 0
    vmem += 2 * (tm * tn * 2 + tm * LANES * 4) + m * LANES * 4 if emit_norm else 0
    vmem += tm * tn * 4
    outs = pl.pallas_call(
        functools.partial(_wsmm_kernel, tm=tm, ck=ck, nc=nc, n_sub=n_sub, tail=tail, relu2=relu2, has_res=has_res, has_scale=has_scale,
                          has_rope=has_rope, emit_norm=emit_norm, d_norm=float(x.shape[1]) if has_scale else None),
        grid=(nj + 1, ni),
        in_specs=in_specs,
        out_specs=out_specs,
        out_shape=out_shape,
        scratch_shapes=scratch,
        compiler_params=pltpu.CompilerParams(
            dimension_semantics=("arbitrary", "arbitrary"),
            vmem_limit_bytes=_vmem_limit(vmem)),
        name=name,
    )(*args)
    return outs if (emit_norm or tail) else outs[0]


def _sigmoid(x):
    return 1.0 / (1.0 + jnp.exp(-x))


def _layernorm_silu(c, g, b):
    mu = jnp.mean(c, axis=-1, keepdims=True)
    xc = c - mu
    var = jnp.mean(xc * xc, axis=-1, keepdims=True)
    y = xc * lax.rsqrt(var + EPS) * g + b
    return y * _sigmoid(y)


def _rope_tables(pos):
    half = ROT_DIM // 2
    inv_freq = jnp.exp(-math.log(ROPE_THETA) * 2.0 * jnp.arange(half, dtype=F32) / ROT_DIM)
    ang = pos.astype(F32)[:, None] * inv_freq[None, :]
    cos, sin = jnp.cos(ang), jnp.sin(ang)
    s = pos.shape[0]
    ones = jnp.ones((s, HEAD_DIM - ROT_DIM), F32)
    zeros = jnp.zeros((s, HEAD_DIM - ROT_DIM), F32)
    zh = jnp.zeros((s, half), F32)
    cos_t = jnp.concatenate([cos, cos, ones], axis=-1)
    sin_lo = jnp.concatenate([-sin, zh, zeros], axis=-1)
    sin_hi = jnp.concatenate([zh, sin, zeros], axis=-1)
    return cos_t, sin_lo, sin_hi


def _pool_group(zbuf, base, rows, g, w, pos0, wpool_ref, pscale_ref, window_sum=None):
    ls = slice(g * POOL_GROUP, (g + 1) * POOL_GROUP)
    cur = zbuf[base:base + rows, ls]
    s = window_sum
    if s is None:
        s = cur
        for i in range(1, w):
            s = s + zbuf[base - i:base - i + rows, ls]
    pos = pos0 + lax.broadcasted_iota(jnp.int32, (rows, 1), 0)
    cnt = jnp.minimum(pos + 1, w).astype(F32)
    p = (s / cnt - cur).astype(BF16)
    y = jnp.dot(p, wpool_ref[g].astype(BF16), preferred_element_type=F32)
    return y * pscale_ref[:, ls]


def _convpool_kernel(proj_ref, chist_ref, phist_ref, wconv_ref, bconv_ref, lng_ref, lnb_ref,
                     wpool_ref, pscale_ref, mix_in_ref, mix_ref, cnew_ref, pnew_ref, ubuf, ush, zbuf, cbuf,
                     *sbufs, tile, n_tiles):
    del mix_in_ref
    t = pl.program_id(1)

    @pl.when(t == 0)
    def _():
        ubuf[0:CONV_HALO - CONV_BUF, :] = jnp.zeros((CONV_HALO - CONV_BUF, D_CONV), F32)
        ubuf[CONV_HALO - CONV_BUF:CONV_HALO, :] = chist_ref[0]
        zbuf[0:POOL_HALO - POOL_BUF, :] = jnp.zeros((POOL_HALO - POOL_BUF, D_POOL), F32)
        zbuf[POOL_HALO - POOL_BUF:POOL_HALO, :] = phist_ref[0]
        for sb in sbufs:
            sb[0:POOL_HALO - POOL_SPAN, :] = jnp.zeros((POOL_HALO - POOL_SPAN, sb.shape[1]), F32)

    a = proj_ref[:, OFF_A:OFF_A + D_CONV]
    b = proj_ref[:, OFF_B:OFF_B + D_CONV]
    ubuf[CONV_HALO:CONV_HALO + tile, :] = a * _sigmoid(b)
    zbuf[POOL_HALO:POOL_HALO + tile, :] = proj_ref[:, OFF_Z:OFF_Z + D_POOL]

    span = tile + CONV_HALO - SUBLANES
    for s in range(1, SUBLANES):
        ush[s - 1, 0:span, :] = ubuf[s:s + span, :]
    rc, lc = 64, 256
    first = CONV_HALO - CONV_BUF
    for r0 in range(0, tile, rc):
        for c in range(D_CONV // lc):
            ls = slice(c * lc, (c + 1) * lc)
            acc = None
            for k in range(CONV_WIDTH):
                q, s = divmod(first + k, SUBLANES)
                rows = slice(r0 + q * SUBLANES, r0 + q * SUBLANES + rc)
                src = ubuf[rows, ls] if s == 0 else ush[s - 1, rows, ls]
                term = src * wconv_ref[k:k + 1, ls]
                acc = term if acc is None else acc + term
            cbuf[r0:r0 + rc, ls] = acc + bconv_ref[:, ls]
    c = _layernorm_silu(cbuf[...], lng_ref[...], lnb_ref[...])
    mix_ref[:, 0:D_CONV] = c.astype(mix_ref.dtype)

    base, lo = POOL_HALO, POOL_HALO - POOL_SPAN
    src, src_col = zbuf, 0
    for g, w in enumerate(POOL_WINDOWS):
        half = w // 2
        gs = slice(g * POOL_GROUP - src_col, (g + 1) * POOL_GROUP - src_col)
        s = src[base:base + tile, gs] + src[base - half:base - half + tile, gs]
        y = _pool_group(zbuf, base, tile, g, w, t * tile, wpool_ref, pscale_ref, window_sum=s)
        mix_ref[:, D_CONV + g * POOL_GROUP:D_CONV + (g + 1) * POOL_GROUP] = y.astype(mix_ref.dtype)
        if g + 1 < len(POOL_WINDOWS):
            assert POOL_WINDOWS[g + 1] == 2 * w
            rest = slice((g + 1) * POOL_GROUP - src_col, D_POOL - src_col)
            nxt = sbufs[g]
            nxt[lo:base + tile, :] = src[lo:base + tile, rest] + src[lo - half:base + tile - half, rest]
            src, src_col = nxt, (g + 1) * POOL_GROUP

    @pl.when(t == n_tiles - 1)
    def _():
        cnew_ref[0] = ubuf[CONV_HALO + tile - CONV_BUF:CONV_HALO + tile, :]
        pnew_ref[0] = zbuf[POOL_HALO + tile - POOL_BUF:POOL_HALO + tile, :]

    ubuf[0:CONV_HALO, :] = ubuf[tile:tile + CONV_HALO, :]
    zbuf[0:POOL_HALO, :] = zbuf[tile:tile + POOL_HALO, :]


def _convpool_prompt(proj, mix, chist, phist, wconv, bconv, lng, lnb, wpool, pscale, *, batch, seq):
    tile = _divisor_tile(seq, 256, 2 * CONV_HALO)
    n_tiles = seq // tile
    row = lambda b, t: (b * n_tiles + t, 0)
    const2 = lambda b, t: (0, 0)
    per_batch = lambda b, t: (b, 0, 0)
    width = OFF_Q
    return pl.pallas_call(
        functools.partial(_convpool_kernel, tile=tile, n_tiles=n_tiles),
        grid=(batch, n_tiles),
        in_specs=[pl.BlockSpec((tile, width), row),
                  pl.BlockSpec((1, CONV_BUF, D_CONV), per_batch),
                  pl.BlockSpec((1, POOL_BUF, D_POOL), per_batch),
                  pl.BlockSpec((CONV_WIDTH, D_CONV), const2),
                  pl.BlockSpec((1, D_CONV), const2),
                  pl.BlockSpec((1, D_CONV), const2),
                  pl.BlockSpec((1, D_CONV), const2),
                  pl.BlockSpec((len(POOL_WINDOWS), POOL_GROUP, POOL_GROUP), lambda b, t: (0, 0, 0)),
                  pl.BlockSpec((1, D_POOL), const2),
                  pl.BlockSpec(memory_space=pl.ANY)],
        out_specs=[pl.BlockSpec((tile, D_CONV + D_POOL), row),
                   pl.BlockSpec((1, CONV_BUF, D_CONV), per_batch),
                   pl.BlockSpec((1, POOL_BUF, D_POOL), per_batch)],
        out_shape=[jax.ShapeDtypeStruct(mix.shape, mix.dtype),
                   jax.ShapeDtypeStruct((batch, CONV_BUF, D_CONV), F32),
                   jax.ShapeDtypeStruct((batch, POOL_BUF, D_POOL), F32)],
        scratch_shapes=[pltpu.VMEM((CONV_HALO + tile, D_CONV), F32),
                        pltpu.VMEM((SUBLANES - 1, CONV_HALO + tile, D_CONV), F32),
                        pltpu.VMEM((POOL_HALO + tile, D_POOL), F32),
                        pltpu.VMEM((tile, D_CONV), F32)]
                       + [pltpu.VMEM((POOL_HALO + tile, D_POOL - g * POOL_GROUP), F32)
                          for g in range(1, len(POOL_WINDOWS))],
        input_output_aliases={9: 0},
        compiler_params=pltpu.CompilerParams(dimension_semantics=("arbitrary", "arbitrary")),
        name="convpool_prompt",
    )(proj, chist, phist, wconv, bconv.reshape(1, -1), lng.reshape(1, -1), lnb.reshape(1, -1),
      wpool, pscale.reshape(1, -1), mix)


BLOCK_UNROLL = 4


def _softmax_sink_pv(s, sink_col, vv):
    m = jnp.maximum(jnp.max(s, axis=-1, keepdims=True), sink_col)
    e = jnp.exp(s - m)
    den = jnp.sum(e, axis=-1, keepdims=True) + jnp.exp(sink_col - m)
    return jnp.dot(e.astype(BF16), vv, preferred_element_type=F32) / den


def _attn_prompt_kernel(sink_ref, q_ref, k_ref, v_ref, mix_in_ref, o_ref, knew_ref, vnew_ref, kbuf, vbuf, bias,
                        *, seq):
    del mix_in_ref
    kh = pl.program_id(1)
    blk = ATTN_BLOCK
    n_blocks = seq // blk
    rows_all = Q_PER_KV * blk

    kbuf[0:blk, :] = jnp.zeros((blk, HEAD_DIM), BF16)
    vbuf[0:blk, :] = jnp.zeros((blk, HEAD_DIM), BF16)

    row = lax.broadcasted_iota(jnp.int32, (rows_all, 1), 0)
    qi = jnp.bitwise_and(row, blk - 1)
    kj = lax.broadcasted_iota(jnp.int32, (1, 2 * blk), 1)
    band = (kj >= qi) & (kj <= qi + WINDOW)
    bias[1] = jnp.where(band, 0.0, -jnp.inf)
    bias[0] = jnp.where(band & (kj >= blk), 0.0, -jnp.inf)
    sink_col = jnp.zeros((rows_all, 1), F32)
    for g in range(Q_PER_KV):
        sink_col = jnp.where((row >= g * blk) & (row < (g + 1) * blk), sink_ref[kh * Q_PER_KV + g], sink_col)
    scale = HEAD_DIM ** -0.5

    last = slice(seq - KV_BUF, seq)
    knew_ref[0] = k_ref[last, :]
    vnew_ref[0] = v_ref[last, :]
    kbuf[blk:blk + seq, :] = k_ref[...].astype(BF16)
    vbuf[blk:blk + seq, :] = v_ref[...].astype(BF16)

    def block(n, carry):
        r0 = pl.multiple_of(n * blk, blk)
        rows = pl.ds(r0, blk)
        q4 = jnp.concatenate(
            [q_ref[rows, g * HEAD_DIM:(g + 1) * HEAD_DIM].astype(BF16) for g in range(Q_PER_KV)], axis=0)
        kk = kbuf[pl.ds(r0, 2 * blk), :]
        vv = vbuf[pl.ds(r0, 2 * blk), :]
        s = lax.dot_general(q4, kk, (((1,), (1,)), ((), ())), preferred_element_type=F32) * scale
        s = s + bias[jnp.minimum(n, 1)]
        o = _softmax_sink_pv(s, sink_col, vv)
        for g in range(Q_PER_KV):
            o_ref[rows, g * HEAD_DIM:(g + 1) * HEAD_DIM] = o[g * blk:(g + 1) * blk, :].astype(o_ref.dtype)
        return carry

    lax.fori_loop(0, n_blocks, block, 0, unroll=BLOCK_UNROLL)


def _attn_prompt(proj, mix, sinks, *, batch, seq):
    m = proj.shape[0]
    gw = Q_PER_KV * HEAD_DIM
    return pl.pallas_call(
        functools.partial(_attn_prompt_kernel, seq=seq),
        grid=(batch, N_KV_HEADS),
        in_specs=[pl.BlockSpec(memory_space=pltpu.SMEM),
                  pl.BlockSpec((seq, gw), lambda b, h: (b, OFF_Q // gw + h)),
                  pl.BlockSpec((seq, HEAD_DIM), lambda b, h: (b, OFF_K // HEAD_DIM + h)),
                  pl.BlockSpec((seq, HEAD_DIM), lambda b, h: (b, OFF_V // HEAD_DIM + h)),
                  pl.BlockSpec(memory_space=pl.ANY)],
        out_specs=[pl.BlockSpec((seq, gw), lambda b, h: (b, (D_CONV + D_POOL) // gw + h)),
                   pl.BlockSpec((1, KV_BUF, HEAD_DIM), lambda b, h: (b, 0, h)),
                   pl.BlockSpec((1, KV_BUF, HEAD_DIM), lambda b, h: (b, 0, h))],
        out_shape=[jax.ShapeDtypeStruct((m, D_MIX), BF16),
                   jax.ShapeDtypeStruct((batch, KV_BUF, D_KV), F32),
                   jax.ShapeDtypeStruct((batch, KV_BUF, D_KV), F32)],
        scratch_shapes=[pltpu.VMEM((seq + ATTN_BLOCK, HEAD_DIM), BF16),
                        pltpu.VMEM((seq + ATTN_BLOCK, HEAD_DIM), BF16),
                        pltpu.VMEM((2, Q_PER_KV * ATTN_BLOCK, 2 * ATTN_BLOCK), F32)],
        input_output_aliases={4: 0},
        compiler_params=pltpu.CompilerParams(dimension_semantics=("arbitrary", "arbitrary")),
        name="attn_prompt",
    )(sinks, proj, proj, proj, mix)


KEYS_PAD = 256


def _sample_one(bb, row0, sink_ref, proj_ref, chist_ref, phist_ref, ck_ref, cv_ref, wconv_ref, bconv_ref,
                lng_ref, lnb_ref, wpool_ref, pscale_ref,
                cnew_ref, pnew_ref, knew_ref, vnew_ref, fbuf, pbuf, kkbuf, vvbuf, qbuf, obuf, mstep, ts):
    out_rows = slice(row0, row0 + ts)
    a = proj_ref[bb, :, OFF_A:OFF_A + D_CONV]
    b = proj_ref[bb, :, OFF_B:OFF_B + D_CONV]
    fbuf[0:CONV_BUF, :] = chist_ref[bb]
    fbuf[CONV_BUF:CONV_BUF + ts, :] = a * _sigmoid(b)
    acc = fbuf[0:ts, :] * wconv_ref[0:1, :]
    for k in range(1, CONV_WIDTH):
        acc = acc + fbuf[k:k + ts, :] * wconv_ref[k:k + 1, :]
    mstep[out_rows, 0:D_CONV] = _layernorm_silu(acc + bconv_ref[...], lng_ref[...], lnb_ref[...])
    cnew_ref[bb] = fbuf[ts:ts + CONV_BUF, :]

    pbuf[0:POOL_BUF, :] = phist_ref[bb]
    pbuf[POOL_BUF:POOL_BUF + ts, :] = proj_ref[bb, :, OFF_Z:OFF_Z + D_POOL]
    for g, w in enumerate(POOL_WINDOWS):
        y = _pool_group(pbuf, POOL_BUF, ts, g, w, PAST_LEN, wpool_ref, pscale_ref)
        mstep[out_rows, D_CONV + g * POOL_GROUP:D_CONV + (g + 1) * POOL_GROUP] = y
    pnew_ref[bb] = pbuf[ts:ts + POOL_BUF, :]

    for h in range(N_KV_HEADS):
        hs = slice(h * HEAD_DIM, (h + 1) * HEAD_DIM)
        kkbuf[0:KV_BUF, hs] = ck_ref[bb, :, h, :]
        vvbuf[0:KV_BUF, hs] = cv_ref[bb, :, h, :]
    kkbuf[KV_BUF:KV_BUF + ts, :] = proj_ref[bb, :, OFF_K:OFF_K + D_KV]
    vvbuf[KV_BUF:KV_BUF + ts, :] = proj_ref[bb, :, OFF_V:OFF_V + D_KV]
    pad = KEYS_PAD - KV_BUF - ts
    kkbuf[KV_BUF + ts:KEYS_PAD, :] = jnp.zeros((pad, D_KV), F32)
    vvbuf[KV_BUF + ts:KEYS_PAD, :] = jnp.zeros((pad, D_KV), F32)
    for h in range(N_KV_HEADS):
        hs = slice(h * HEAD_DIM, (h + 1) * HEAD_DIM)
        knew_ref[bb, :, h, :] = kkbuf[ts:ts + KV_BUF, hs]
        vnew_ref[bb, :, h, :] = vvbuf[ts:ts + KV_BUF, hs]

    rows_all = Q_PER_KV * ts
    row = lax.broadcasted_iota(jnp.int32, (rows_all, 1), 0)
    tq = row % ts
    kj = lax.broadcasted_iota(jnp.int32, (1, KEYS_PAD), 1)
    valid = (kj >= tq + (KV_BUF - WINDOW)) & (kj <= tq + KV_BUF) & (kj < KV_BUF + ts)
    scale = HEAD_DIM ** -0.5
    for kh in range(N_KV_HEADS):
        hs = slice(kh * HEAD_DIM, (kh + 1) * HEAD_DIM)
        sink_col = jnp.zeros((rows_all, 1), F32)
        for g in range(Q_PER_KV):
            h = kh * Q_PER_KV + g
            qbuf[g * ts:(g + 1) * ts, :] = proj_ref[bb, :, OFF_Q + h * HEAD_DIM:OFF_Q + (h + 1) * HEAD_DIM]
            sink_col = jnp.where((row >= g * ts) & (row < (g + 1) * ts), sink_ref[h], sink_col)
        q = qbuf[...].astype(BF16)
        kk = kkbuf[:, hs].astype(BF16)
        vv = vvbuf[:, hs].astype(BF16)
        s = lax.dot_general(q, kk, (((1,), (1,)), ((), ())), preferred_element_type=F32) * scale
        s = jnp.where(valid, s, -jnp.inf)
        obuf[...] = _softmax_sink_pv(s, sink_col, vv)
        for g in range(Q_PER_KV):
            h = kh * Q_PER_KV + g
            col = D_CONV + D_POOL + h * HEAD_DIM
            mstep[out_rows, col:col + HEAD_DIM] = obuf[g * ts:(g + 1) * ts, :]


def _sample_kernel(*refs, steps, seqs, n_steps, n_alias):
    (sink_ref, proj_ref, chist_ref, phist_ref, ck_ref, cv_ref, wconv_ref, bconv_ref,
     lng_ref, lnb_ref, wpool_ref, pscale_ref) = refs[:12]
    (mix_ref, cnew_ref, pnew_ref, knew_ref, vnew_ref, fbuf, pbuf, kkbuf, vvbuf, qbuf, obuf,
     mstep, mixacc) = refs[12 + n_alias:]
    step = pl.program_id(0)
    for bb in range(seqs):
        _sample_one(bb, bb * steps, sink_ref, proj_ref, chist_ref, phist_ref, ck_ref, cv_ref, wconv_ref,
                    bconv_ref, lng_ref, lnb_ref, wpool_ref, pscale_ref,
                    cnew_ref, pnew_ref, knew_ref, vnew_ref, fbuf.at[bb], pbuf.at[bb], kkbuf.at[bb], vvbuf.at[bb],
                    qbuf.at[bb], obuf.at[bb], mstep, steps)
    rows = seqs * steps
    mixacc[pl.ds(pl.multiple_of(step * rows, rows), rows), :] = mstep[...]

    @pl.when(step == n_steps - 1)
    def _():
        mix_ref[...] = mixacc[...].astype(mix_ref.dtype)


def _sample_mixers(proj, mix, chist, phist, ck, cv, sinks, wconv, bconv, lng, lnb, wpool, pscale,
                   *, row0, layer, states):
    nb, ts, _ = proj.shape
    depth = ck.shape[0]
    assert SUBLANES % ts == 0
    seqs = SUBLANES // ts
    assert nb % seqs == 0 and row0 % (nb * ts) == 0 and (nb * ts) % SUBLANES_BF16 == 0
    n_steps = nb // seqs
    per_step = lambda s: (s, 0, 0)
    const2 = lambda s: (0, 0)
    vec = lambda n: pl.BlockSpec((1, n), const2)
    stacked = lambda s: (layer, s, 0, 0)
    stacked_kv = lambda s: (layer, s, 0, 0, 0)
    kv_block = (None, seqs, KV_BUF, N_KV_HEADS, HEAD_DIM)
    aliased = [mix] + list(states)
    n_fixed = 12
    return pl.pallas_call(
        functools.partial(_sample_kernel, steps=ts, seqs=seqs, n_steps=n_steps, n_alias=len(aliased)),
        grid=(n_steps,),
        in_specs=[pl.BlockSpec(memory_space=pltpu.SMEM),
                  pl.BlockSpec((seqs, ts, D_IN), per_step),
                  pl.BlockSpec((None, seqs, CONV_BUF, D_CONV), stacked),
                  pl.BlockSpec((None, seqs, POOL_BUF, D_POOL), stacked),
                  pl.BlockSpec(kv_block, stacked_kv),
                  pl.BlockSpec(kv_block, stacked_kv),
                  pl.BlockSpec((CONV_WIDTH, D_CONV), const2),
                  vec(D_CONV), vec(D_CONV), vec(D_CONV),
                  pl.BlockSpec((len(POOL_WINDOWS), POOL_GROUP, POOL_GROUP), lambda s: (0, 0, 0)),
                  vec(D_POOL)] + [pl.BlockSpec(memory_space=pl.ANY)] * len(aliased),
        out_specs=[pl.BlockSpec((nb * ts, D_MIX), lambda s: (row0 // (nb * ts), 0)),
                   pl.BlockSpec((None, seqs, CONV_BUF, D_CONV), stacked),
                   pl.BlockSpec((None, seqs, POOL_BUF, D_POOL), stacked),
                   pl.BlockSpec(kv_block, stacked_kv),
                   pl.BlockSpec(kv_block, stacked_kv)],
        out_shape=[jax.ShapeDtypeStruct(mix.shape, mix.dtype),
                   jax.ShapeDtypeStruct((depth, nb, CONV_BUF, D_CONV), F32),
                   jax.ShapeDtypeStruct((depth, nb, POOL_BUF, D_POOL), F32),
                   jax.ShapeDtypeStruct((depth, nb, KV_BUF, N_KV_HEADS, HEAD_DIM), F32),
                   jax.ShapeDtypeStruct((depth, nb, KV_BUF, N_KV_HEADS, HEAD_DIM), F32)],
        scratch_shapes=[pltpu.VMEM((seqs, CONV_BUF + 2 * ts + 2, D_CONV), F32),
                        pltpu.VMEM((seqs, POOL_BUF + 2 * ts + 1, D_POOL), F32),
                        pltpu.VMEM((seqs, KEYS_PAD, D_KV), F32),
                        pltpu.VMEM((seqs, KEYS_PAD, D_KV), F32),
                        pltpu.VMEM((seqs, Q_PER_KV * ts, HEAD_DIM), F32),
                        pltpu.VMEM((seqs, Q_PER_KV * ts, HEAD_DIM), F32),
                        pltpu.VMEM((seqs * ts, D_MIX), F32),
                        pltpu.VMEM((nb * ts, D_MIX), F32)],
        input_output_aliases={n_fixed + a: a for a in range(len(aliased))},
        compiler_params=pltpu.CompilerParams(dimension_semantics=("arbitrary",)),
        name="sample_mixers",
    )(sinks, proj, chist, phist, ck, cv, wconv, bconv.reshape(1, -1), lng.reshape(1, -1), lnb.reshape(1, -1),
      wpool, pscale.reshape(1, -1), *aliased)


def kernel(x_prompt, x_sample, state_conv, state_pool, cache_k, cache_v, norm_mix, w_in, w_conv, b_conv,
           ln_conv_g, ln_conv_b, w_pool, pool_scale, q_norm, k_norm, attn_sinks, w_out, norm_ffn, w_up, w_down):
    bp, sp, d_model = x_prompt.shape
    bs, ts, _ = x_sample.shape
    depth = w_in.shape[0]
    assert w_in.shape[2] == D_IN and sp % ATTN_BLOCK == 0 and KV_BUF == WINDOW
    mp, ms = bp * sp, bs * ts

    tab_p = _rope_tables(jnp.arange(sp, dtype=jnp.int32))
    tab_s = _rope_tables(PAST_LEN + jnp.arange(ts, dtype=jnp.int32))
    tables = tuple(jnp.concatenate([jnp.tile(tp, (bp, 1)), jnp.tile(tsm, (bs, 1))], axis=0)
                   for tp, tsm in zip(tab_p, tab_s))
    head_mask = jnp.zeros((D_IN,), F32).at[OFF_Q:OFF_V].set(1.0)
    zero_conv = jnp.zeros((bp, CONV_BUF, D_CONV), F32)
    zero_pool = jnp.zeros((bp, POOL_BUF, D_POOL), F32)
    n_chunks = w_up.shape[2] // d_model

    mix = jnp.zeros((mp + ms, D_MIX), BF16)
    states_s = [jnp.zeros(a.shape, F32) for a in (state_conv, state_pool, cache_k, cache_v)]

    outs = [[] for _ in range(4)]
    x, xg, ssq = _merge_prenorm(x_prompt.reshape(mp, d_model), x_sample.reshape(ms, d_model), norm_mix[0])
    for l in range(depth):
        mixer_w = (w_conv[l], b_conv[l], ln_conv_g[l], ln_conv_b[l], w_pool[l], pool_scale[l])
        head_gain = jnp.concatenate([jnp.ones((OFF_Q,), F32), jnp.tile(q_norm[l], N_HEADS),
                                     jnp.tile(k_norm[l], N_KV_HEADS), jnp.ones((D_KV,), F32)])

        proj = _wsmm(xg, w_in, l, ssq=ssq, rope=(head_gain, head_mask) + tables, tm_target=ROPE_TM, n_sub=ROPE_SUB,
                     name="in_proj")
        mix, conv_p, pool_p = _convpool_prompt(proj, mix, zero_conv, zero_pool, *mixer_w, batch=bp, seq=sp)
        mix, k_p, v_p = _attn_prompt(proj, mix, attn_sinks[l], batch=bp, seq=sp)
        mix, *states_s = _sample_mixers(
            proj[mp:].reshape(bs, ts, D_IN), mix, state_conv, state_pool, cache_k, cache_v,
            attn_sinks[l], *mixer_w, row0=mp, layer=l, states=states_s)
        x, xg, ssq = _wsmm(mix, w_out, l, res=x, norm_gain=norm_ffn[l], tm_target=NORM_TM, name="out_proj")

        up = _wsmm(xg, w_up, l, ssq=ssq, relu2=True, out_dtype=BF16, name="ffn_up")
        for c in range(n_chunks):
            if c == n_chunks - 1 and l + 1 < depth:
                x, xg, ssq = _wsmm(up, w_down, l, k_chunk=c, res=x, norm_gain=norm_mix[l + 1], tm_target=NORM_TM,
                                   name="ffn_down")
            else:
                last = c == n_chunks - 1
                x = _wsmm(up, w_down, l, k_chunk=c, res=x, tm_target=RES_TM, split_rows=mp if last else None,
                          name="ffn_down")

        for lst, val in zip(outs, (conv_p, pool_p, k_p, v_p)):
            lst.append(val)

    conv_p, pool_p, k_p, v_p = [jnp.stack(o) for o in outs]
    conv_s, pool_s, k_s, v_s = states_s
    kv_shape_p = (depth, bp, KV_BUF, N_KV_HEADS, HEAD_DIM)
    y_prompt, y_sample = x
    return (y_prompt.reshape(bp, sp, d_model), y_sample.reshape(bs, ts, d_model),
            conv_p, pool_p, k_p.reshape(kv_shape_p), v_p.reshape(kv_shape_p), conv_s, pool_s, k_s, v_s)
```

```python
import functools
import math

import jax
import jax.numpy as jnp
from jax import lax
from jax.experimental import pallas as pl
from jax.experimental.pallas import tpu as pltpu

F32 = jnp.float32
BF16 = jnp.bfloat16

HEAD_DIM = 128
N_KV_HEADS = 4
Q_PER_KV = 4
N_HEADS = N_KV_HEADS * Q_PER_KV
D_ATTN = N_HEADS * HEAD_DIM
D_KV = N_KV_HEADS * HEAD_DIM
D_CONV = 1024
D_POOL = 1024
D_MIX = D_CONV + D_POOL + D_ATTN
ROT_DIM = HEAD_DIM // 4
ROPE_THETA = 500000.0
WINDOW = 128
ATTN_BLOCK = 128
KV_BUF = 128
CONV_WIDTH = 31
CONV_BUF = CONV_WIDTH - 1
POOL_WINDOWS = (2, 4, 8, 16)
POOL_GROUP = D_POOL // len(POOL_WINDOWS)
POOL_BUF = max(POOL_WINDOWS) - 1
PAST_LEN = 8192
EPS = 1e-6

OFF_A = 0
OFF_B = D_CONV
OFF_Z = 2 * D_CONV
OFF_Q = 2 * D_CONV + D_POOL
OFF_K = OFF_Q + D_ATTN
OFF_V = OFF_K + D_KV
D_IN = OFF_V + D_KV

V7X_VMEM_BYTES = 64 * 1024 * 1024
LANES = 128
SUBLANES = 8
SUBLANES_BF16 = 16

ROPE_TM = 1040
ROPE_SUB = 5
RES_TM = 832
NORM_TM = 640
CONV_HALO = 32
POOL_HALO = 24
POOL_SPAN = 16
assert POOL_WINDOWS[0] == 2 and POOL_SPAN > POOL_BUF and POOL_HALO - POOL_SPAN >= max(POOL_WINDOWS) // 2


def _divisor_tile(n, target, multiple):
    best = None
    for d in range(multiple, min(n, target) + 1, multiple):
        if n % d == 0:
            best = d
    return n if best is None else best


def _vmem_limit(nbytes):
    return int(min(V7X_VMEM_BYTES - 2 * 1024 * 1024, nbytes + 6 * 1024 * 1024))


def _fold_lanes(sq):
    out = sq[:, 0:LANES]
    for c in range(1, sq.shape[1] // LANES):
        out = out + sq[:, c * LANES:(c + 1) * LANES]
    return out


def _row_rstd(ssq, d_norm):
    return lax.rsqrt(jnp.sum(ssq, axis=-1, keepdims=True) / d_norm + EPS)


def _merge_prenorm_kernel(xp_ref, xs_ref, g_ref, x_ref, xg_ref, ssq_ref, *, n_prompt_tiles):
    x = jnp.where(pl.program_id(0) < n_prompt_tiles, xp_ref[...], xs_ref[...])
    x_ref[...] = x
    xg_ref[...] = (x * g_ref[...]).astype(xg_ref.dtype)
    ssq_ref[...] = _fold_lanes(x * x)


def _merge_prenorm(x_prompt, x_sample, g):
    (mp, d), ms = x_prompt.shape, x_sample.shape[0]
    assert mp % ms == 0 and ms % SUBLANES_BF16 == 0
    npt = mp // ms
    m = mp + ms
    row = lambda i: (i, 0)
    return pl.pallas_call(
        functools.partial(_merge_prenorm_kernel, n_prompt_tiles=npt),
        grid=(npt + 1,),
        in_specs=[pl.BlockSpec((ms, d), lambda i: (jnp.minimum(i, npt - 1), 0)),
                  pl.BlockSpec((ms, d), lambda i: (0, 0)),
                  pl.BlockSpec((1, d), lambda i: (0, 0))],
        out_specs=[pl.BlockSpec((ms, d), row), pl.BlockSpec((ms, d), row), pl.BlockSpec((ms, LANES), row)],
        out_shape=[jax.ShapeDtypeStruct((m, d), F32),
                   jax.ShapeDtypeStruct((m, d), BF16),
                   jax.ShapeDtypeStruct((m, LANES), F32)],
        compiler_params=pltpu.CompilerParams(dimension_semantics=("arbitrary",)),
        name="merge_prenorm",
    )(x_prompt, x_sample, g.reshape(1, d))


def _head_norm_rope(acc, gain_ref, mask_ref, cos, sin_lo, sin_hi):
    half = ROT_DIM // 2
    cols = []
    for c in range(acc.shape[1] // HEAD_DIM):
        cs = slice(c * HEAD_DIM, (c + 1) * HEAD_DIM)
        a = acc[:, cs]
        y = a * lax.rsqrt(jnp.mean(a * a, axis=-1, keepdims=True) + EPS) * gain_ref[:, cs]
        up = pltpu.roll(y, HEAD_DIM - half, 1)
        down = pltpu.roll(y, half, 1)
        cols.append(jnp.where(mask_ref[:, cs] > 0.0, y * cos + up * sin_lo + down * sin_hi, a))
    return jnp.concatenate(cols, axis=1)


def _wsmm_kernel(*refs, tm, ck, nc, n_sub, tail, relu2, has_res, has_scale, has_rope, emit_norm, d_norm):
    it = iter(refs)
    x_ref, w_ref = next(it), next(it)
    res_ref = next(it) if has_res else None
    ssq_in_ref = next(it) if has_scale else None
    rope_refs = [next(it) for _ in range(5)] if has_rope else None
    g_ref = next(it) if emit_norm else None
    o_ref = next(it)
    tail_ref = next(it) if tail else None
    xg_ref, ssq_out_ref = (next(it), next(it)) if emit_norm else (None, None)
    wbf = next(it)
    ssq_acc = next(it) if emit_norm else None
    p = pl.program_id(0)
    i = pl.program_id(1)
    rows = pl.ds(pl.multiple_of(i * tm, tm), tm)

    def cast_chunk():
        wrows = pl.ds(pl.multiple_of(jnp.minimum(i, nc - 1) * ck, ck), ck)
        wbf[p % 2, wrows, :] = w_ref[...].astype(BF16)

    @pl.when(p == 0)
    def _():
        cast_chunk()
        if emit_norm:
            ssq_acc[rows, :] = jnp.zeros((tm, LANES), F32)

    def sub_tile(r):
        acc = jnp.dot(x_ref[r, :], wbf[(p + 1) % 2], preferred_element_type=F32)
        if has_scale:
            acc = acc * _row_rstd(ssq_in_ref[r, :], d_norm)
        if has_rope:
            gain_ref, mask_ref, cos_ref, slo_ref, shi_ref = rope_refs
            acc = _head_norm_rope(acc, gain_ref, mask_ref, cos_ref[r, :], slo_ref[r, :], shi_ref[r, :])
        if relu2:
            acc = jnp.square(jnp.maximum(acc, 0.0))
        if has_res:
            acc = res_ref[r, :] + acc
        o_ref[r, :] = acc.astype(o_ref.dtype)
        if tail:
            tail_ref[...] = acc[tail[0]:tail[0] + tail[1], :].astype(tail_ref.dtype)
        if emit_norm:
            xg_ref[r, :] = (acc * g_ref[...]).astype(xg_ref.dtype)
            acc_rows = pl.ds(pl.multiple_of(i * tm + r.start, SUBLANES_BF16), r.stop - r.start)
            total = ssq_acc[acc_rows, :] + _fold_lanes(acc * acc)
            ssq_acc[acc_rows, :] = total
            ssq_out_ref[r, :] = total

    @pl.when(p > 0)
    def _():
        sub = tm // n_sub
        for h in range(n_sub):
            sub_tile(slice(h * sub, (h + 1) * sub))
        cast_chunk()


def _wsmm(x, w, layer, *, k_chunk=0, res=None, ssq=None, rope=None, norm_gain=None, relu2=False, out_dtype=F32,
          tm_target=1040, n_sub=1, split_rows=None, tn=1024, tk=4096, name="wsmm"):
    m = x.shape[0]
    n = w.shape[2]
    assert x.shape[1] % tk == 0 and w.shape[1] == x.shape[1] and n % tn == 0
    tm = _divisor_tile(m, tm_target, SUBLANES_BF16 * n_sub)
    ni, nj = m // tm, n // tn
    nc = max(c for c in range(1, ni + 1) if tk % c == 0 and (tk // c) % SUBLANES_BF16 == 0)
    ck = tk // nc
    has_res, has_scale, emit_norm = res is not None, ssq is not None, norm_gain is not None
    has_rope = rope is not None

    def x_map(p, i):
        return jnp.where(p == 0, 0, i), k_chunk

    def w_map(p, i):
        return layer, k_chunk * nc + jnp.where(p < nj, jnp.minimum(i, nc - 1), nc - 1), jnp.minimum(p, nj - 1)

    def o_map(p, i):
        return jnp.where(p == 0, 0, i), jnp.maximum(p - 1, 0)

    def row_map(p, i):
        return jnp.where(p == 0, 0, i), 0

    in_specs = [pl.BlockSpec((tm, tk), x_map), pl.BlockSpec((None, ck, tn), w_map)]
    args = [x, w]
    if has_res:
        in_specs.append(pl.BlockSpec((tm, tn), o_map))
        args.append(res)
    if has_scale:
        in_specs.append(pl.BlockSpec((tm, LANES), row_map))
        args.append(ssq)
    col_vec = pl.BlockSpec((1, tn), lambda p, i: (0, jnp.maximum(p - 1, 0)))
    if has_rope:
        gains, mask, cos_t, sin_lo, sin_hi = rope
        in_specs += [col_vec, col_vec] + [pl.BlockSpec((tm, HEAD_DIM), row_map)] * 3
        args += [gains.reshape(1, n), mask.reshape(1, n), cos_t, sin_lo, sin_hi]
    out_specs = [pl.BlockSpec((tm, tn), o_map)]
    out_shape = [jax.ShapeDtypeStruct((m, n), out_dtype)]
    tail = None
    if split_rows is not None:
        n_tail = m - split_rows
        tail = (split_rows - (ni - 1) * tm, n_tail)
        assert n_sub == 1 and 0 < tail[0] and tail[0] % SUBLANES == 0 and n_tail % SUBLANES == 0
        out_shape = [jax.ShapeDtypeStruct((split_rows, n), out_dtype), jax.ShapeDtypeStruct((n_tail, n), out_dtype)]
        out_specs.append(pl.BlockSpec((n_tail, tn), lambda p, i: (0, jnp.maximum(p - 1, 0))))
    scratch = [pltpu.VMEM((2, tk, tn), BF16)]
    if emit_norm:
        in_specs.append(col_vec)
        args.append(norm_gain.reshape(1, n))
        out_specs += [pl.BlockSpec((tm, tn), o_map),
                      pl.BlockSpec((tm, LANES), lambda p, i: (jnp.where(p == nj, i, 0), 0))]
        out_shape += [jax.ShapeDtypeStruct((m, n), BF16), jax.ShapeDtypeStruct((m, LANES), F32)]
        scratch.append(pltpu.VMEM((m, LANES), F32))
    out_bytes = jnp.dtype(out_dtype).itemsize
    vmem = 2 * (tm * tk * 2 + ck * tn * 4 + tm * tn * out_bytes) + 2 * tk * tn * 2
    vmem += 2 * tm * tn * 4 if has_res else 0
    vmem += 2 * tm * LANES * 4 if has_scale else 0
    vmem += 6 * tm * HEAD_DIM * 4 if has_rope else 0
    vmem += 2 * (tm * tn * 2 + tm * LANES * 4) + m * LANES * 4 if emit_norm else 0
    vmem += tm * tn * 4
    outs = pl.pallas_call(
        functools.partial(_wsmm_kernel, tm=tm, ck=ck, nc=nc, n_sub=n_sub, tail=tail, relu2=relu2, has_res=has_res, has_scale=has_scale,
                          has_rope=has_rope, emit_norm=emit_norm, d_norm=float(x.shape[1]) if has_scale else None),
        grid=(nj + 1, ni),
        in_specs=in_specs,
        out_specs=out_specs,
        out_shape=out_shape,
        scratch_shapes=scratch,
        compiler_params=pltpu.CompilerParams(
            dimension_semantics=("arbitrary", "arbitrary"),
            vmem_limit_bytes=_vmem_limit(vmem)),
        name=name,
    )(*args)
    return outs if (emit_norm or tail) else outs[0]


def _sigmoid(x):
    return 1.0 / (1.0 + jnp.exp(-x))


def _layernorm_silu(c, g, b):
    mu = jnp.mean(c, axis=-1, keepdims=True)
    xc = c - mu
    var = jnp.mean(xc * xc, axis=-1, keepdims=True)
    y = xc * lax.rsqrt(var + EPS) * g + b
    return y * _sigmoid(y)


def _rope_tables(pos):
    half = ROT_DIM // 2
    inv_freq = jnp.exp(-math.log(ROPE_THETA) * 2.0 * jnp.arange(half, dtype=F32) / ROT_DIM)
    ang = pos.astype(F32)[:, None] * inv_freq[None, :]
    cos, sin = jnp.cos(ang), jnp.sin(ang)
    s = pos.shape[0]
    ones = jnp.ones((s, HEAD_DIM - ROT_DIM), F32)
    zeros = jnp.zeros((s, HEAD_DIM - ROT_DIM), F32)
    zh = jnp.zeros((s, half), F32)
    cos_t = jnp.concatenate([cos, cos, ones], axis=-1)
    sin_lo = jnp.concatenate([-sin, zh, zeros], axis=-1)
    sin_hi = jnp.concatenate([zh, sin, zeros], axis=-1)
    return cos_t, sin_lo, sin_hi


def _pool_group(zbuf, base, rows, g, w, pos0, wpool_ref, pscale_ref, window_sum=None):
    ls = slice(g * POOL_GROUP, (g + 1) * POOL_GROUP)
    cur = zbuf[base:base + rows, ls]
    s = window_sum
    if s is None:
        s = cur
        for i in range(1, w):
            s = s + zbuf[base - i:base - i + rows, ls]
    pos = pos0 + lax.broadcasted_iota(jnp.int32, (rows, 1), 0)
    cnt = jnp.minimum(pos + 1, w).astype(F32)
    p = (s / cnt - cur).astype(BF16)
    y = jnp.dot(p, wpool_ref[g].astype(BF16), preferred_element_type=F32)
    return y * pscale_ref[:, ls]


def _convpool_kernel(proj_ref, chist_ref, phist_ref, wconv_ref, bconv_ref, lng_ref, lnb_ref,
                     wpool_ref, pscale_ref, mix_in_ref, mix_ref, cnew_ref, pnew_ref, ubuf, ush, zbuf, cbuf,
                     *sbufs, tile, n_tiles):
    del mix_in_ref
    t = pl.program_id(1)

    @pl.when(t == 0)
    def _():
        ubuf[0:CONV_HALO - CONV_BUF, :] = jnp.zeros((CONV_HALO - CONV_BUF, D_CONV), F32)
        ubuf[CONV_HALO - CONV_BUF:CONV_HALO, :] = chist_ref[0]
        zbuf[0:POOL_HALO - POOL_BUF, :] = jnp.zeros((POOL_HALO - POOL_BUF, D_POOL), F32)
        zbuf[POOL_HALO - POOL_BUF:POOL_HALO, :] = phist_ref[0]
        for sb in sbufs:
            sb[0:POOL_HALO - POOL_SPAN, :] = jnp.zeros((POOL_HALO - POOL_SPAN, sb.shape[1]), F32)

    a = proj_ref[:, OFF_A:OFF_A + D_CONV]
    b = proj_ref[:, OFF_B:OFF_B + D_CONV]
    ubuf[CONV_HALO:CONV_HALO + tile, :] = a * _sigmoid(b)
    zbuf[POOL_HALO:POOL_HALO + tile, :] = proj_ref[:, OFF_Z:OFF_Z + D_POOL]

    span = tile + CONV_HALO - SUBLANES
    for s in range(1, SUBLANES):
        ush[s - 1, 0:span, :] = ubuf[s:s + span, :]
    rc, lc = 64, 256
    first = CONV_HALO - CONV_BUF
    for r0 in range(0, tile, rc):
        for c in range(D_CONV // lc):
            ls = slice(c * lc, (c + 1) * lc)
            acc = None
            for k in range(CONV_WIDTH):
                q, s = divmod(first + k, SUBLANES)
                rows = slice(r0 + q * SUBLANES, r0 + q * SUBLANES + rc)
                src = ubuf[rows, ls] if s == 0 else ush[s - 1, rows, ls]
                term = src.reshape(rc // SUBLANES, SUBLANES, lc) * wconv_ref[k, :, ls][None]
                acc = term if acc is None else acc + term
            cbuf[r0:r0 + rc, ls] = acc.reshape(rc, lc) + bconv_ref[:, ls]
    c = _layernorm_silu(cbuf[...], lng_ref[...], lnb_ref[...])
    mix_ref[:, 0:D_CONV] = c.astype(mix_ref.dtype)

    base, lo = POOL_HALO, POOL_HALO - POOL_SPAN
    src, src_col = zbuf, 0
    for g, w in enumerate(POOL_WINDOWS):
        half = w // 2
        gs = slice(g * POOL_GROUP - src_col, (g + 1) * POOL_GROUP - src_col)
        s = src[base:base + tile, gs] + src[base - half:base - half + tile, gs]
        y = _pool_group(zbuf, base, tile, g, w, t * tile, wpool_ref, pscale_ref, window_sum=s)
        mix_ref[:, D_CONV + g * POOL_GROUP:D_CONV + (g + 1) * POOL_GROUP] = y.astype(mix_ref.dtype)
        if g + 1 < len(POOL_WINDOWS):
            assert POOL_WINDOWS[g + 1] == 2 * w
            rest = slice((g + 1) * POOL_GROUP - src_col, D_POOL - src_col)
            nxt = sbufs[g]
            nxt[lo:base + tile, :] = src[lo:base + tile, rest] + src[lo - half:base + tile - half, rest]
            src, src_col = nxt, (g + 1) * POOL_GROUP

    @pl.when(t == n_tiles - 1)
    def _():
        cnew_ref[0] = ubuf[CONV_HALO + tile - CONV_BUF:CONV_HALO + tile, :]
        pnew_ref[0] = zbuf[POOL_HALO + tile - POOL_BUF:POOL_HALO + tile, :]

    ubuf[0:CONV_HALO, :] = ubuf[tile:tile + CONV_HALO, :]
    zbuf[0:POOL_HALO, :] = zbuf[tile:tile + POOL_HALO, :]


def _convpool_prompt(proj, mix, chist, phist, wconv, bconv, lng, lnb, wpool, pscale, *, batch, seq):
    tile = _divisor_tile(seq, 256, 2 * CONV_HALO)
    n_tiles = seq // tile
    row = lambda b, t: (b * n_tiles + t, 0)
    const2 = lambda b, t: (0, 0)
    per_batch = lambda b, t: (b, 0, 0)
    width = OFF_Q
    return pl.pallas_call(
        functools.partial(_convpool_kernel, tile=tile, n_tiles=n_tiles),
        grid=(batch, n_tiles),
        in_specs=[pl.BlockSpec((tile, width), row),
                  pl.BlockSpec((1, CONV_BUF, D_CONV), per_batch),
                  pl.BlockSpec((1, POOL_BUF, D_POOL), per_batch),
                  pl.BlockSpec((CONV_WIDTH, SUBLANES, D_CONV), lambda b, t: (0, 0, 0)),
                  pl.BlockSpec((1, D_CONV), const2),
                  pl.BlockSpec((1, D_CONV), const2),
                  pl.BlockSpec((1, D_CONV), const2),
                  pl.BlockSpec((len(POOL_WINDOWS), POOL_GROUP, POOL_GROUP), lambda b, t: (0, 0, 0)),
                  pl.BlockSpec((1, D_POOL), const2),
                  pl.BlockSpec(memory_space=pl.ANY)],
        out_specs=[pl.BlockSpec((tile, D_CONV + D_POOL), row),
                   pl.BlockSpec((1, CONV_BUF, D_CONV), per_batch),
                   pl.BlockSpec((1, POOL_BUF, D_POOL), per_batch)],
        out_shape=[jax.ShapeDtypeStruct(mix.shape, mix.dtype),
                   jax.ShapeDtypeStruct((batch, CONV_BUF, D_CONV), F32),
                   jax.ShapeDtypeStruct((batch, POOL_BUF, D_POOL), F32)],
        scratch_shapes=[pltpu.VMEM((CONV_HALO + tile, D_CONV), F32),
                        pltpu.VMEM((SUBLANES - 1, CONV_HALO + tile, D_CONV), F32),
                        pltpu.VMEM((POOL_HALO + tile, D_POOL), F32),
                        pltpu.VMEM((tile, D_CONV), F32)]
                       + [pltpu.VMEM((POOL_HALO + tile, D_POOL - g * POOL_GROUP), F32)
                          for g in range(1, len(POOL_WINDOWS))],
        input_output_aliases={9: 0},
        compiler_params=pltpu.CompilerParams(dimension_semantics=("arbitrary", "arbitrary")),
        name="convpool_prompt",
    )(proj, chist, phist, jnp.broadcast_to(wconv[:, None, :], (CONV_WIDTH, SUBLANES, D_CONV)),
      bconv.reshape(1, -1), lng.reshape(1, -1), lnb.reshape(1, -1), wpool, pscale.reshape(1, -1), mix)


BLOCK_UNROLL = 8


def _softmax_sink_pv(s, sink_col, vv):
    m = jnp.maximum(jnp.max(s, axis=-1, keepdims=True), sink_col)
    e = jnp.exp(s - m)
    den = jnp.sum(e, axis=-1, keepdims=True) + jnp.exp(sink_col - m)
    return jnp.dot(e.astype(BF16), vv, preferred_element_type=F32) / den


def _attn_prompt_kernel(sink_ref, q_ref, k_ref, v_ref, mix_in_ref, o_ref, knew_ref, vnew_ref, kbuf, vbuf, bias,
                        *, seq):
    del mix_in_ref
    kh = pl.program_id(1)
    blk = ATTN_BLOCK
    n_blocks = seq // blk
    rows_all = Q_PER_KV * blk

    kbuf[0:blk, :] = jnp.zeros((blk, HEAD_DIM), BF16)
    vbuf[0:blk, :] = jnp.zeros((blk, HEAD_DIM), BF16)

    row = lax.broadcasted_iota(jnp.int32, (rows_all, 1), 0)
    qi = jnp.bitwise_and(row, blk - 1)
    kj = lax.broadcasted_iota(jnp.int32, (1, 2 * blk), 1)
    band = (kj >= qi) & (kj <= qi + WINDOW)
    bias[1] = jnp.where(band, 0.0, -jnp.inf)
    bias[0] = jnp.where(band & (kj >= blk), 0.0, -jnp.inf)
    sink_col = jnp.zeros((rows_all, 1), F32)
    for g in range(Q_PER_KV):
        sink_col = jnp.where((row >= g * blk) & (row < (g + 1) * blk), sink_ref[kh * Q_PER_KV + g], sink_col)
    scale = HEAD_DIM ** -0.5

    last = slice(seq - KV_BUF, seq)
    knew_ref[0] = k_ref[last, :]
    vnew_ref[0] = v_ref[last, :]
    kbuf[blk:blk + seq, :] = k_ref[...].astype(BF16)
    vbuf[blk:blk + seq, :] = v_ref[...].astype(BF16)

    def block(n, carry):
        r0 = pl.multiple_of(n * blk, blk)
        rows = pl.ds(r0, blk)
        q4 = jnp.concatenate(
            [q_ref[rows, g * HEAD_DIM:(g + 1) * HEAD_DIM].astype(BF16) for g in range(Q_PER_KV)], axis=0)
        kk = kbuf[pl.ds(r0, 2 * blk), :]
        vv = vbuf[pl.ds(r0, 2 * blk), :]
        s = lax.dot_general(q4, kk, (((1,), (1,)), ((), ())), preferred_element_type=F32) * scale
        s = s + bias[jnp.minimum(n, 1)]
        o = _softmax_sink_pv(s, sink_col, vv)
        for g in range(Q_PER_KV):
            o_ref[rows, g * HEAD_DIM:(g + 1) * HEAD_DIM] = o[g * blk:(g + 1) * blk, :].astype(o_ref.dtype)
        return carry

    lax.fori_loop(0, n_blocks, block, 0, unroll=BLOCK_UNROLL)


def _attn_prompt(proj, mix, sinks, *, batch, seq):
    m = proj.shape[0]
    gw = Q_PER_KV * HEAD_DIM
    return pl.pallas_call(
        functools.partial(_attn_prompt_kernel, seq=seq),
        grid=(batch, N_KV_HEADS),
        in_specs=[pl.BlockSpec(memory_space=pltpu.SMEM),
                  pl.BlockSpec((seq, gw), lambda b, h: (b, OFF_Q // gw + h)),
                  pl.BlockSpec((seq, HEAD_DIM), lambda b, h: (b, OFF_K // HEAD_DIM + h)),
                  pl.BlockSpec((seq, HEAD_DIM), lambda b, h: (b, OFF_V // HEAD_DIM + h)),
                  pl.BlockSpec(memory_space=pl.ANY)],
        out_specs=[pl.BlockSpec((seq, gw), lambda b, h: (b, (D_CONV + D_POOL) // gw + h)),
                   pl.BlockSpec((1, KV_BUF, HEAD_DIM), lambda b, h: (b, 0, h)),
                   pl.BlockSpec((1, KV_BUF, HEAD_DIM), lambda b, h: (b, 0, h))],
        out_shape=[jax.ShapeDtypeStruct((m, D_MIX), BF16),
                   jax.ShapeDtypeStruct((batch, KV_BUF, D_KV), F32),
                   jax.ShapeDtypeStruct((batch, KV_BUF, D_KV), F32)],
        scratch_shapes=[pltpu.VMEM((seq + ATTN_BLOCK, HEAD_DIM), BF16),
                        pltpu.VMEM((seq + ATTN_BLOCK, HEAD_DIM), BF16),
                        pltpu.VMEM((2, Q_PER_KV * ATTN_BLOCK, 2 * ATTN_BLOCK), F32)],
        input_output_aliases={4: 0},
        compiler_params=pltpu.CompilerParams(dimension_semantics=("arbitrary", "arbitrary")),
        name="attn_prompt",
    )(sinks, proj, proj, proj, mix)


KEYS_PAD = 256


def _sample_one(bb, row0, sink_ref, proj_ref, chist_ref, phist_ref, ck_ref, cv_ref, wconv_ref, bconv_ref,
                lng_ref, lnb_ref, wpool_ref, pscale_ref,
                cnew_ref, pnew_ref, knew_ref, vnew_ref, fbuf, pbuf, kkbuf, vvbuf, qbuf, obuf, mstep, ts):
    out_rows = slice(row0, row0 + ts)
    a = proj_ref[bb, :, OFF_A:OFF_A + D_CONV]
    b = proj_ref[bb, :, OFF_B:OFF_B + D_CONV]
    fbuf[0:CONV_BUF, :] = chist_ref[bb]
    fbuf[CONV_BUF:CONV_BUF + ts, :] = a * _sigmoid(b)
    acc = fbuf[0:ts, :] * wconv_ref[0:1, :]
    for k in range(1, CONV_WIDTH):
        acc = acc + fbuf[k:k + ts, :] * wconv_ref[k:k + 1, :]
    mstep[out_rows, 0:D_CONV] = _layernorm_silu(acc + bconv_ref[...], lng_ref[...], lnb_ref[...])
    cnew_ref[bb] = fbuf[ts:ts + CONV_BUF, :]

    pbuf[0:POOL_BUF, :] = phist_ref[bb]
    pbuf[POOL_BUF:POOL_BUF + ts, :] = proj_ref[bb, :, OFF_Z:OFF_Z + D_POOL]
    for g, w in enumerate(POOL_WINDOWS):
        y = _pool_group(pbuf, POOL_BUF, ts, g, w, PAST_LEN, wpool_ref, pscale_ref)
        mstep[out_rows, D_CONV + g * POOL_GROUP:D_CONV + (g + 1) * POOL_GROUP] = y
    pnew_ref[bb] = pbuf[ts:ts + POOL_BUF, :]

    for h in range(N_KV_HEADS):
        hs = slice(h * HEAD_DIM, (h + 1) * HEAD_DIM)
        kkbuf[0:KV_BUF, hs] = ck_ref[bb, :, h, :]
        vvbuf[0:KV_BUF, hs] = cv_ref[bb, :, h, :]
    kkbuf[KV_BUF:KV_BUF + ts, :] = proj_ref[bb, :, OFF_K:OFF_K + D_KV]
    vvbuf[KV_BUF:KV_BUF + ts, :] = proj_ref[bb, :, OFF_V:OFF_V + D_KV]
    pad = KEYS_PAD - KV_BUF - ts
    kkbuf[KV_BUF + ts:KEYS_PAD, :] = jnp.zeros((pad, D_KV), F32)
    vvbuf[KV_BUF + ts:KEYS_PAD, :] = jnp.zeros((pad, D_KV), F32)
    for h in range(N_KV_HEADS):
        hs = slice(h * HEAD_DIM, (h + 1) * HEAD_DIM)
        knew_ref[bb, :, h, :] = kkbuf[ts:ts + KV_BUF, hs]
        vnew_ref[bb, :, h, :] = vvbuf[ts:ts + KV_BUF, hs]

    rows_all = Q_PER_KV * ts
    row = lax.broadcasted_iota(jnp.int32, (rows_all, 1), 0)
    tq = row % ts
    kj = lax.broadcasted_iota(jnp.int32, (1, KEYS_PAD), 1)
    valid = (kj >= tq + (KV_BUF - WINDOW)) & (kj <= tq + KV_BUF) & (kj < KV_BUF + ts)
    scale = HEAD_DIM ** -0.5
    for kh in range(N_KV_HEADS):
        hs = slice(kh * HEAD_DIM, (kh + 1) * HEAD_DIM)
        sink_col = jnp.zeros((rows_all, 1), F32)
        for g in range(Q_PER_KV):
            h = kh * Q_PER_KV + g
            qbuf[g * ts:(g + 1) * ts, :] = proj_ref[bb, :, OFF_Q + h * HEAD_DIM:OFF_Q + (h + 1) * HEAD_DIM]
            sink_col = jnp.where((row >= g * ts) & (row < (g + 1) * ts), sink_ref[h], sink_col)
        q = qbuf[...].astype(BF16)
        kk = kkbuf[:, hs].astype(BF16)
        vv = vvbuf[:, hs].astype(BF16)
        s = lax.dot_general(q, kk, (((1,), (1,)), ((), ())), preferred_element_type=F32) * scale
        s = jnp.where(valid, s, -jnp.inf)
        obuf[...] = _softmax_sink_pv(s, sink_col, vv)
        for g in range(Q_PER_KV):
            h = kh * Q_PER_KV + g
            col = D_CONV + D_POOL + h * HEAD_DIM
            mstep[out_rows, col:col + HEAD_DIM] = obuf[g * ts:(g + 1) * ts, :]


def _sample_kernel(*refs, steps, seqs, n_steps, n_alias):
    (sink_ref, proj_ref, chist_ref, phist_ref, ck_ref, cv_ref, wconv_ref, bconv_ref,
     lng_ref, lnb_ref, wpool_ref, pscale_ref) = refs[:12]
    (mix_ref, cnew_ref, pnew_ref, knew_ref, vnew_ref, fbuf, pbuf, kkbuf, vvbuf, qbuf, obuf,
     mstep, mixacc) = refs[12 + n_alias:]
    step = pl.program_id(0)
    for bb in range(seqs):
        _sample_one(bb, bb * steps, sink_ref, proj_ref, chist_ref, phist_ref, ck_ref, cv_ref, wconv_ref,
                    bconv_ref, lng_ref, lnb_ref, wpool_ref, pscale_ref,
                    cnew_ref, pnew_ref, knew_ref, vnew_ref, fbuf.at[bb], pbuf.at[bb], kkbuf.at[bb], vvbuf.at[bb],
                    qbuf.at[bb], obuf.at[bb], mstep, steps)
    rows = seqs * steps
    mixacc[pl.ds(pl.multiple_of(step * rows, rows), rows), :] = mstep[...]

    @pl.when(step == n_steps - 1)
    def _():
        mix_ref[...] = mixacc[...].astype(mix_ref.dtype)


def _sample_mixers(proj, mix, chist, phist, ck, cv, sinks, wconv, bconv, lng, lnb, wpool, pscale,
                   *, row0, layer, states):
    nb, ts, _ = proj.shape
    depth = ck.shape[0]
    assert SUBLANES % ts == 0
    seqs = SUBLANES // ts
    assert nb % seqs == 0 and row0 % (nb * ts) == 0 and (nb * ts) % SUBLANES_BF16 == 0
    n_steps = nb // seqs
    per_step = lambda s: (s, 0, 0)
    const2 = lambda s: (0, 0)
    vec = lambda n: pl.BlockSpec((1, n), const2)
    stacked = lambda s: (layer, s, 0, 0)
    stacked_kv = lambda s: (layer, s, 0, 0, 0)
    kv_block = (None, seqs, KV_BUF, N_KV_HEADS, HEAD_DIM)
    aliased = [mix] + list(states)
    n_fixed = 12
    return pl.pallas_call(
        functools.partial(_sample_kernel, steps=ts, seqs=seqs, n_steps=n_steps, n_alias=len(aliased)),
        grid=(n_steps,),
        in_specs=[pl.BlockSpec(memory_space=pltpu.SMEM),
                  pl.BlockSpec((seqs, ts, D_IN), per_step),
                  pl.BlockSpec((None, seqs, CONV_BUF, D_CONV), stacked),
                  pl.BlockSpec((None, seqs, POOL_BUF, D_POOL), stacked),
                  pl.BlockSpec(kv_block, stacked_kv),
                  pl.BlockSpec(kv_block, stacked_kv),
                  pl.BlockSpec((CONV_WIDTH, D_CONV), const2),
                  vec(D_CONV), vec(D_CONV), vec(D_CONV),
                  pl.BlockSpec((len(POOL_WINDOWS), POOL_GROUP, POOL_GROUP), lambda s: (0, 0, 0)),
                  vec(D_POOL)] + [pl.BlockSpec(memory_space=pl.ANY)] * len(aliased),
        out_specs=[pl.BlockSpec((nb * ts, D_MIX), lambda s: (row0 // (nb * ts), 0)),
                   pl.BlockSpec((None, seqs, CONV_BUF, D_CONV), stacked),
                   pl.BlockSpec((None, seqs, POOL_BUF, D_POOL), stacked),
                   pl.BlockSpec(kv_block, stacked_kv),
                   pl.BlockSpec(kv_block, stacked_kv)],
        out_shape=[jax.ShapeDtypeStruct(mix.shape, mix.dtype),
                   jax.ShapeDtypeStruct((depth, nb, CONV_BUF, D_CONV), F32),
                   jax.ShapeDtypeStruct((depth, nb, POOL_BUF, D_POOL), F32),
                   jax.ShapeDtypeStruct((depth, nb, KV_BUF, N_KV_HEADS, HEAD_DIM), F32),
                   jax.ShapeDtypeStruct((depth, nb, KV_BUF, N_KV_HEADS, HEAD_DIM), F32)],
        scratch_shapes=[pltpu.VMEM((seqs, CONV_BUF + 2 * ts + 2, D_CONV), F32),
                        pltpu.VMEM((seqs, POOL_BUF + 2 * ts + 1, D_POOL), F32),
                        pltpu.VMEM((seqs, KEYS_PAD, D_KV), F32),
                        pltpu.VMEM((seqs, KEYS_PAD, D_KV), F32),
                        pltpu.VMEM((seqs, Q_PER_KV * ts, HEAD_DIM), F32),
                        pltpu.VMEM((seqs, Q_PER_KV * ts, HEAD_DIM), F32),
                        pltpu.VMEM((seqs * ts, D_MIX), F32),
                        pltpu.VMEM((nb * ts, D_MIX), F32)],
        input_output_aliases={n_fixed + a: a for a in range(len(aliased))},
        compiler_params=pltpu.CompilerParams(dimension_semantics=("arbitrary",)),
        name="sample_mixers",
    )(sinks, proj, chist, phist, ck, cv, wconv, bconv.reshape(1, -1), lng.reshape(1, -1), lnb.reshape(1, -1),
      wpool, pscale.reshape(1, -1), *aliased)


def kernel(x_prompt, x_sample, state_conv, state_pool, cache_k, cache_v, norm_mix, w_in, w_conv, b_conv,
           ln_conv_g, ln_conv_b, w_pool, pool_scale, q_norm, k_norm, attn_sinks, w_out, norm_ffn, w_up, w_down):
    bp, sp, d_model = x_prompt.shape
    bs, ts, _ = x_sample.shape
    depth = w_in.shape[0]
    assert w_in.shape[2] == D_IN and sp % ATTN_BLOCK == 0 and KV_BUF == WINDOW
    mp, ms = bp * sp, bs * ts

    tab_p = _rope_tables(jnp.arange(sp, dtype=jnp.int32))
    tab_s = _rope_tables(PAST_LEN + jnp.arange(ts, dtype=jnp.int32))
    tables = tuple(jnp.concatenate([jnp.tile(tp, (bp, 1)), jnp.tile(tsm, (bs, 1))], axis=0)
                   for tp, tsm in zip(tab_p, tab_s))
    head_mask = jnp.zeros((D_IN,), F32).at[OFF_Q:OFF_V].set(1.0)
    zero_conv = jnp.zeros((bp, CONV_BUF, D_CONV), F32)
    zero_pool = jnp.zeros((bp, POOL_BUF, D_POOL), F32)
    n_chunks = w_up.shape[2] // d_model

    mix = jnp.zeros((mp + ms, D_MIX), BF16)
    states_s = [jnp.zeros(a.shape, F32) for a in (state_conv, state_pool, cache_k, cache_v)]

    outs = [[] for _ in range(4)]
    x, xg, ssq = _merge_prenorm(x_prompt.reshape(mp, d_model), x_sample.reshape(ms, d_model), norm_mix[0])
    for l in range(depth):
        mixer_w = (w_conv[l], b_conv[l], ln_conv_g[l], ln_conv_b[l], w_pool[l], pool_scale[l])
        head_gain = jnp.concatenate([jnp.ones((OFF_Q,), F32), jnp.tile(q_norm[l], N_HEADS),
                                     jnp.tile(k_norm[l], N_KV_HEADS), jnp.ones((D_KV,), F32)])

        proj = _wsmm(xg, w_in, l, ssq=ssq, rope=(head_gain, head_mask) + tables, tm_target=ROPE_TM, n_sub=ROPE_SUB,
                     name="in_proj")
        mix, conv_p, pool_p = _convpool_prompt(proj, mix, zero_conv, zero_pool, *mixer_w, batch=bp, seq=sp)
        mix, k_p, v_p = _attn_prompt(proj, mix, attn_sinks[l], batch=bp, seq=sp)
        mix, *states_s = _sample_mixers(
            proj[mp:].reshape(bs, ts, D_IN), mix, state_conv, state_pool, cache_k, cache_v,
            attn_sinks[l], *mixer_w, row0=mp, layer=l, states=states_s)
        x, xg, ssq = _wsmm(mix, w_out, l, res=x, norm_gain=norm_ffn[l], tm_target=NORM_TM, name="out_proj")

        up = _wsmm(xg, w_up, l, ssq=ssq, relu2=True, out_dtype=BF16, name="ffn_up")
        for c in range(n_chunks):
            if c == n_chunks - 1 and l + 1 < depth:
                x, xg, ssq = _wsmm(up, w_down, l, k_chunk=c, res=x, norm_gain=norm_mix[l + 1], tm_target=NORM_TM,
                                   name="ffn_down")
            else:
                last = c == n_chunks - 1
                x = _wsmm(up, w_down, l, k_chunk=c, res=x, tm_target=RES_TM, split_rows=mp if last else None,
                          name="ffn_down")

        for lst, val in zip(outs, (conv_p, pool_p, k_p, v_p)):
            lst.append(val)

    conv_p, pool_p, k_p, v_p = [jnp.stack(o) for o in outs]
    conv_s, pool_s, k_s, v_s = states_s
    kv_shape_p = (depth, bp, KV_BUF, N_KV_HEADS, HEAD_DIM)
    y_prompt, y_sample = x
    return (y_prompt.reshape(bp, sp, d_model), y_sample.reshape(bs, ts, d_model),
            conv_p, pool_p, k_p.reshape(kv_shape_p), v_p.reshape(kv_shape_p), conv_s, pool_s, k_s, v_s)
```

```python
import functools
import math

import jax
import jax.numpy as jnp
from jax import lax
from jax.experimental import pallas as pl
from jax.experimental.pallas import tpu as pltpu

F32 = jnp.float32
BF16 = jnp.bfloat16

HEAD_DIM = 128
N_KV_HEADS = 4
Q_PER_KV = 4
N_HEADS = N_KV_HEADS * Q_PER_KV
D_ATTN = N_HEADS * HEAD_DIM
D_KV = N_KV_HEADS * HEAD_DIM
D_CONV = 1024
D_POOL = 1024
D_MIX = D_CONV + D_POOL + D_ATTN
ROT_DIM = HEAD_DIM // 4
ROPE_THETA = 500000.0
WINDOW = 128
ATTN_BLOCK = 128
KV_BUF = 128
CONV_WIDTH = 31
CONV_BUF = CONV_WIDTH - 1
POOL_WINDOWS = (2, 4, 8, 16)
POOL_GROUP = D_POOL // len(POOL_WINDOWS)
POOL_BUF = max(POOL_WINDOWS) - 1
PAST_LEN = 8192
EPS = 1e-6

OFF_A = 0
OFF_B = D_CONV
OFF_Z = 2 * D_CONV
OFF_Q = 2 * D_CONV + D_POOL
OFF_K = OFF_Q + D_ATTN
OFF_V = OFF_K + D_KV
D_IN = OFF_V + D_KV

V7X_VMEM_BYTES = 64 * 1024 * 1024
LANES = 128
SUBLANES = 8
SUBLANES_BF16 = 16

ROPE_TM = 1040
ROPE_SUB = 5
RES_TM = 832
NORM_TM = 640
CONV_HALO = 32
POOL_HALO = 24
POOL_SPAN = 16
assert POOL_WINDOWS[0] == 2 and POOL_SPAN > POOL_BUF and POOL_HALO - POOL_SPAN >= max(POOL_WINDOWS) // 2


def _divisor_tile(n, target, multiple):
    best = None
    for d in range(multiple, min(n, target) + 1, multiple):
        if n % d == 0:
            best = d
    return n if best is None else best


def _vmem_limit(nbytes):
    return int(min(V7X_VMEM_BYTES - 2 * 1024 * 1024, nbytes + 6 * 1024 * 1024))


def _fold_lanes(sq):
    out = sq[:, 0:LANES]
    for c in range(1, sq.shape[1] // LANES):
        out = out + sq[:, c * LANES:(c + 1) * LANES]
    return out


def _row_rstd(ssq, d_norm):
    return lax.rsqrt(jnp.sum(ssq, axis=-1, keepdims=True) / d_norm + EPS)


def _merge_prenorm_kernel(xp_ref, xs_ref, g_ref, x_ref, xg_ref, ssq_ref, *, n_prompt_tiles):
    x = jnp.where(pl.program_id(0) < n_prompt_tiles, xp_ref[...], xs_ref[...])
    x_ref[...] = x
    xg_ref[...] = (x * g_ref[...]).astype(xg_ref.dtype)
    ssq_ref[...] = _fold_lanes(x * x)


def _merge_prenorm(x_prompt, x_sample, g):
    (mp, d), ms = x_prompt.shape, x_sample.shape[0]
    assert mp % ms == 0 and ms % SUBLANES_BF16 == 0
    npt = mp // ms
    m = mp + ms
    row = lambda i: (i, 0)
    return pl.pallas_call(
        functools.partial(_merge_prenorm_kernel, n_prompt_tiles=npt),
        grid=(npt + 1,),
        in_specs=[pl.BlockSpec((ms, d), lambda i: (jnp.minimum(i, npt - 1), 0)),
                  pl.BlockSpec((ms, d), lambda i: (0, 0)),
                  pl.BlockSpec((1, d), lambda i: (0, 0))],
        out_specs=[pl.BlockSpec((ms, d), row), pl.BlockSpec((ms, d), row), pl.BlockSpec((ms, LANES), row)],
        out_shape=[jax.ShapeDtypeStruct((m, d), F32),
                   jax.ShapeDtypeStruct((m, d), BF16),
                   jax.ShapeDtypeStruct((m, LANES), F32)],
        compiler_params=pltpu.CompilerParams(dimension_semantics=("arbitrary",)),
        name="merge_prenorm",
    )(x_prompt, x_sample, g.reshape(1, d))


def _head_norm_rope(acc, gain_ref, mask_ref, cos, sin_lo, sin_hi):
    half = ROT_DIM // 2
    cols = []
    for c in range(acc.shape[1] // HEAD_DIM):
        cs = slice(c * HEAD_DIM, (c + 1) * HEAD_DIM)
        a = acc[:, cs]
        y = a * lax.rsqrt(jnp.mean(a * a, axis=-1, keepdims=True) + EPS) * gain_ref[:, cs]
        up = pltpu.roll(y, HEAD_DIM - half, 1)
        down = pltpu.roll(y, half, 1)
        cols.append(jnp.where(mask_ref[:, cs] > 0.0, y * cos + up * sin_lo + down * sin_hi, a))
    return jnp.concatenate(cols, axis=1)


def _wsmm_kernel(*refs, tm, ck, nc, n_sub, tail, rope_tiles, relu2, has_res, has_scale, has_rope, emit_norm, d_norm):
    it = iter(refs)
    x_ref, w_ref = next(it), next(it)
    res_ref = next(it) if has_res else None
    ssq_in_ref = next(it) if has_scale else None
    rope_refs = [next(it) for _ in range(5)] if has_rope else None
    g_ref = next(it) if emit_norm else None
    o_ref = next(it)
    tail_ref = next(it) if tail else None
    xg_ref, ssq_out_ref = (next(it), next(it)) if emit_norm else (None, None)
    wbf = next(it)
    ssq_acc = next(it) if emit_norm else None
    p = pl.program_id(0)
    i = pl.program_id(1)
    rows = pl.ds(pl.multiple_of(i * tm, tm), tm)

    def cast_chunk():
        wrows = pl.ds(pl.multiple_of(jnp.minimum(i, nc - 1) * ck, ck), ck)
        wbf[p % 2, wrows, :] = w_ref[...].astype(BF16)

    @pl.when(p == 0)
    def _():
        cast_chunk()
        if emit_norm:
            ssq_acc[rows, :] = jnp.zeros((tm, LANES), F32)

    def sub_tile(r, use_rope):
        acc = jnp.dot(x_ref[r, :], wbf[(p + 1) % 2], preferred_element_type=F32)
        if has_scale:
            acc = acc * _row_rstd(ssq_in_ref[r, :], d_norm)
        if use_rope:
            gain_ref, mask_ref, cos_ref, slo_ref, shi_ref = rope_refs
            acc = _head_norm_rope(acc, gain_ref, mask_ref, cos_ref[r, :], slo_ref[r, :], shi_ref[r, :])
        if relu2:
            acc = jnp.square(jnp.maximum(acc, 0.0))
        if has_res:
            acc = res_ref[r, :] + acc
        o_ref[r, :] = acc.astype(o_ref.dtype)
        if tail:
            tail_ref[...] = acc[tail[0]:tail[0] + tail[1], :].astype(tail_ref.dtype)
        if emit_norm:
            xg_ref[r, :] = (acc * g_ref[...]).astype(xg_ref.dtype)
            acc_rows = pl.ds(pl.multiple_of(i * tm + r.start, SUBLANES_BF16), r.stop - r.start)
            total = ssq_acc[acc_rows, :] + _fold_lanes(acc * acc)
            ssq_acc[acc_rows, :] = total
            ssq_out_ref[r, :] = total

    def row_tile(use_rope):
        sub = tm // n_sub
        for h in range(n_sub):
            sub_tile(slice(h * sub, (h + 1) * sub), use_rope)

    @pl.when(p > 0)
    def _():
        if has_rope:
            hot = (p - 1 >= rope_tiles[0]) & (p - 1 < rope_tiles[1])
            pl.when(hot)(lambda: row_tile(True))
            pl.when(jnp.logical_not(hot))(lambda: row_tile(False))
        else:
            row_tile(False)
        cast_chunk()


def _wsmm(x, w, layer, *, k_chunk=0, res=None, ssq=None, rope=None, norm_gain=None, relu2=False, out_dtype=F32,
          tm_target=1040, n_sub=1, split_rows=None, tn=1024, tk=4096, name="wsmm"):
    m = x.shape[0]
    n = w.shape[2]
    assert x.shape[1] % tk == 0 and w.shape[1] == x.shape[1] and n % tn == 0
    tm = _divisor_tile(m, tm_target, SUBLANES_BF16 * n_sub)
    ni, nj = m // tm, n // tn
    nc = max(c for c in range(1, ni + 1) if tk % c == 0 and (tk // c) % SUBLANES_BF16 == 0)
    ck = tk // nc
    has_res, has_scale, emit_norm = res is not None, ssq is not None, norm_gain is not None
    has_rope = rope is not None
    rope_tiles = (OFF_Q // tn, -(-OFF_V // tn)) if has_rope else None

    def x_map(p, i):
        return jnp.where(p == 0, 0, i), k_chunk

    def w_map(p, i):
        return layer, k_chunk * nc + jnp.where(p < nj, jnp.minimum(i, nc - 1), nc - 1), jnp.minimum(p, nj - 1)

    def o_map(p, i):
        return jnp.where(p == 0, 0, i), jnp.maximum(p - 1, 0)

    def row_map(p, i):
        return jnp.where(p == 0, 0, i), 0

    in_specs = [pl.BlockSpec((tm, tk), x_map), pl.BlockSpec((None, ck, tn), w_map)]
    args = [x, w]
    if has_res:
        in_specs.append(pl.BlockSpec((tm, tn), o_map))
        args.append(res)
    if has_scale:
        in_specs.append(pl.BlockSpec((tm, LANES), row_map))
        args.append(ssq)
    col_vec = pl.BlockSpec((1, tn), lambda p, i: (0, jnp.maximum(p - 1, 0)))
    if has_rope:
        gains, mask, cos_t, sin_lo, sin_hi = rope
        in_specs += [col_vec, col_vec] + [pl.BlockSpec((tm, HEAD_DIM), row_map)] * 3
        args += [gains.reshape(1, n), mask.reshape(1, n), cos_t, sin_lo, sin_hi]
    out_specs = [pl.BlockSpec((tm, tn), o_map)]
    out_shape = [jax.ShapeDtypeStruct((m, n), out_dtype)]
    tail = None
    if split_rows is not None:
        n_tail = m - split_rows
        tail = (split_rows - (ni - 1) * tm, n_tail)
        assert n_sub == 1 and 0 < tail[0] and tail[0] % SUBLANES == 0 and n_tail % SUBLANES == 0
        out_shape = [jax.ShapeDtypeStruct((split_rows, n), out_dtype), jax.ShapeDtypeStruct((n_tail, n), out_dtype)]
        out_specs.append(pl.BlockSpec((n_tail, tn), lambda p, i: (0, jnp.maximum(p - 1, 0))))
    scratch = [pltpu.VMEM((2, tk, tn), BF16)]
    if emit_norm:
        in_specs.append(col_vec)
        args.append(norm_gain.reshape(1, n))
        out_specs += [pl.BlockSpec((tm, tn), o_map),
                      pl.BlockSpec((tm, LANES), lambda p, i: (jnp.where(p == nj, i, 0), 0))]
        out_shape += [jax.ShapeDtypeStruct((m, n), BF16), jax.ShapeDtypeStruct((m, LANES), F32)]
        scratch.append(pltpu.VMEM((m, LANES), F32))
    out_bytes = jnp.dtype(out_dtype).itemsize
    vmem = 2 * (tm * tk * 2 + ck * tn * 4 + tm * tn * out_bytes) + 2 * tk * tn * 2
    vmem += 2 * tm * tn * 4 if has_res else 0
    vmem += 2 * tm * LANES * 4 if has_scale else 0
    vmem += 6 * tm * HEAD_DIM * 4 if has_rope else 0
    vmem += 2 * (tm * tn * 2 + tm * LANES * 4) + m * LANES * 4 if emit_norm else 0
    vmem += tm * tn * 4
    outs = pl.pallas_call(
        functools.partial(_wsmm_kernel, tm=tm, ck=ck, nc=nc, n_sub=n_sub, tail=tail, rope_tiles=rope_tiles, relu2=relu2, has_res=has_res, has_scale=has_scale,
                          has_rope=has_rope, emit_norm=emit_norm, d_norm=float(x.shape[1]) if has_scale else None),
        grid=(nj + 1, ni),
        in_specs=in_specs,
        out_specs=out_specs,
        out_shape=out_shape,
        scratch_shapes=scratch,
        compiler_params=pltpu.CompilerParams(
            dimension_semantics=("arbitrary", "arbitrary"),
            vmem_limit_bytes=_vmem_limit(vmem)),
        name=name,
    )(*args)
    return outs if (emit_norm or tail) else outs[0]


def _sigmoid(x):
    return 1.0 / (1.0 + jnp.exp(-x))


def _layernorm_silu(c, g, b):
    mu = jnp.mean(c, axis=-1, keepdims=True)
    xc = c - mu
    var = jnp.mean(xc * xc, axis=-1, keepdims=True)
    y = xc * lax.rsqrt(var + EPS) * g + b
    return y * _sigmoid(y)


def _rope_tables(pos):
    half = ROT_DIM // 2
    inv_freq = jnp.exp(-math.log(ROPE_THETA) * 2.0 * jnp.arange(half, dtype=F32) / ROT_DIM)
    ang = pos.astype(F32)[:, None] * inv_freq[None, :]
    cos, sin = jnp.cos(ang), jnp.sin(ang)
    s = pos.shape[0]
    ones = jnp.ones((s, HEAD_DIM - ROT_DIM), F32)
    zeros = jnp.zeros((s, HEAD_DIM - ROT_DIM), F32)
    zh = jnp.zeros((s, half), F32)
    cos_t = jnp.concatenate([cos, cos, ones], axis=-1)
    sin_lo = jnp.concatenate([-sin, zh, zeros], axis=-1)
    sin_hi = jnp.concatenate([zh, sin, zeros], axis=-1)
    return cos_t, sin_lo, sin_hi


def _pool_group(zbuf, base, rows, g, w, pos0, wpool_ref, pscale_ref, window_sum=None):
    ls = slice(g * POOL_GROUP, (g + 1) * POOL_GROUP)
    cur = zbuf[base:base + rows, ls]
    s = window_sum
    if s is None:
        s = cur
        for i in range(1, w):
            s = s + zbuf[base - i:base - i + rows, ls]
    pos = pos0 + lax.broadcasted_iota(jnp.int32, (rows, 1), 0)
    cnt = jnp.minimum(pos + 1, w).astype(F32)
    p = (s / cnt - cur).astype(BF16)
    y = jnp.dot(p, wpool_ref[g].astype(BF16), preferred_element_type=F32)
    return y * pscale_ref[:, ls]


def _convpool_kernel(proj_ref, chist_ref, phist_ref, wconv_ref, bconv_ref, lng_ref, lnb_ref,
                     wpool_ref, pscale_ref, mix_in_ref, mix_ref, cnew_ref, pnew_ref, ubuf, ush, zbuf, cbuf,
                     *sbufs, tile, n_tiles):
    del mix_in_ref
    t = pl.program_id(1)

    @pl.when(t == 0)
    def _():
        ubuf[0:CONV_HALO - CONV_BUF, :] = jnp.zeros((CONV_HALO - CONV_BUF, D_CONV), F32)
        ubuf[CONV_HALO - CONV_BUF:CONV_HALO, :] = chist_ref[0]
        zbuf[0:POOL_HALO - POOL_BUF, :] = jnp.zeros((POOL_HALO - POOL_BUF, D_POOL), F32)
        zbuf[POOL_HALO - POOL_BUF:POOL_HALO, :] = phist_ref[0]
        for sb in sbufs:
            sb[0:POOL_HALO - POOL_SPAN, :] = jnp.zeros((POOL_HALO - POOL_SPAN, sb.shape[1]), F32)

    a = proj_ref[:, OFF_A:OFF_A + D_CONV]
    b = proj_ref[:, OFF_B:OFF_B + D_CONV]
    ubuf[CONV_HALO:CONV_HALO + tile, :] = a * _sigmoid(b)
    zbuf[POOL_HALO:POOL_HALO + tile, :] = proj_ref[:, OFF_Z:OFF_Z + D_POOL]

    span = tile + CONV_HALO - SUBLANES
    for s in range(1, SUBLANES):
        ush[s - 1, 0:span, :] = ubuf[s:s + span, :]
    rc, lc = 64, 256
    first = CONV_HALO - CONV_BUF
    for r0 in range(0, tile, rc):
        for c in range(D_CONV // lc):
            ls = slice(c * lc, (c + 1) * lc)
            acc = None
            for k in range(CONV_WIDTH):
                q, s = divmod(first + k, SUBLANES)
                rows = slice(r0 + q * SUBLANES, r0 + q * SUBLANES + rc)
                src = ubuf[rows, ls] if s == 0 else ush[s - 1, rows, ls]
                term = src.reshape(rc // SUBLANES, SUBLANES, lc) * wconv_ref[k, :, ls][None]
                acc = term if acc is None else acc + term
            cbuf[r0:r0 + rc, ls] = acc.reshape(rc, lc) + bconv_ref[:, ls]
    c = _layernorm_silu(cbuf[...], lng_ref[...], lnb_ref[...])
    mix_ref[:, 0:D_CONV] = c.astype(mix_ref.dtype)

    base, lo = POOL_HALO, POOL_HALO - POOL_SPAN
    src, src_col = zbuf, 0
    for g, w in enumerate(POOL_WINDOWS):
        half = w // 2
        gs = slice(g * POOL_GROUP - src_col, (g + 1) * POOL_GROUP - src_col)
        s = src[base:base + tile, gs] + src[base - half:base - half + tile, gs]
        y = _pool_group(zbuf, base, tile, g, w, t * tile, wpool_ref, pscale_ref, window_sum=s)
        mix_ref[:, D_CONV + g * POOL_GROUP:D_CONV + (g + 1) * POOL_GROUP] = y.astype(mix_ref.dtype)
        if g + 1 < len(POOL_WINDOWS):
            assert POOL_WINDOWS[g + 1] == 2 * w
            rest = slice((g + 1) * POOL_GROUP - src_col, D_POOL - src_col)
            nxt = sbufs[g]
            nxt[lo:base + tile, :] = src[lo:base + tile, rest] + src[lo - half:base + tile - half, rest]
            src, src_col = nxt, (g + 1) * POOL_GROUP

    @pl.when(t == n_tiles - 1)
    def _():
        cnew_ref[0] = ubuf[CONV_HALO + tile - CONV_BUF:CONV_HALO + tile, :]
        pnew_ref[0] = zbuf[POOL_HALO + tile - POOL_BUF:POOL_HALO + tile, :]

    ubuf[0:CONV_HALO, :] = ubuf[tile:tile + CONV_HALO, :]
    zbuf[0:POOL_HALO, :] = zbuf[tile:tile + POOL_HALO, :]


def _convpool_prompt(proj, mix, chist, phist, wconv, bconv, lng, lnb, wpool, pscale, *, batch, seq):
    tile = _divisor_tile(seq, 256, 2 * CONV_HALO)
    n_tiles = seq // tile
    row = lambda b, t: (b * n_tiles + t, 0)
    const2 = lambda b, t: (0, 0)
    per_batch = lambda b, t: (b, 0, 0)
    width = OFF_Q
    return pl.pallas_call(
        functools.partial(_convpool_kernel, tile=tile, n_tiles=n_tiles),
        grid=(batch, n_tiles),
        in_specs=[pl.BlockSpec((tile, width), row),
                  pl.BlockSpec((1, CONV_BUF, D_CONV), per_batch),
                  pl.BlockSpec((1, POOL_BUF, D_POOL), per_batch),
                  pl.BlockSpec((CONV_WIDTH, SUBLANES, D_CONV), lambda b, t: (0, 0, 0)),
                  pl.BlockSpec((1, D_CONV), const2),
                  pl.BlockSpec((1, D_CONV), const2),
                  pl.BlockSpec((1, D_CONV), const2),
                  pl.BlockSpec((len(POOL_WINDOWS), POOL_GROUP, POOL_GROUP), lambda b, t: (0, 0, 0)),
                  pl.BlockSpec((1, D_POOL), const2),
                  pl.BlockSpec(memory_space=pl.ANY)],
        out_specs=[pl.BlockSpec((tile, D_CONV + D_POOL), row),
                   pl.BlockSpec((1, CONV_BUF, D_CONV), per_batch),
                   pl.BlockSpec((1, POOL_BUF, D_POOL), per_batch)],
        out_shape=[jax.ShapeDtypeStruct(mix.shape, mix.dtype),
                   jax.ShapeDtypeStruct((batch, CONV_BUF, D_CONV), F32),
                   jax.ShapeDtypeStruct((batch, POOL_BUF, D_POOL), F32)],
        scratch_shapes=[pltpu.VMEM((CONV_HALO + tile, D_CONV), F32),
                        pltpu.VMEM((SUBLANES - 1, CONV_HALO + tile, D_CONV), F32),
                        pltpu.VMEM((POOL_HALO + tile, D_POOL), F32),
                        pltpu.VMEM((tile, D_CONV), F32)]
                       + [pltpu.VMEM((POOL_HALO + tile, D_POOL - g * POOL_GROUP), F32)
                          for g in range(1, len(POOL_WINDOWS))],
        input_output_aliases={9: 0},
        compiler_params=pltpu.CompilerParams(dimension_semantics=("arbitrary", "arbitrary")),
        name="convpool_prompt",
    )(proj, chist, phist, jnp.broadcast_to(wconv[:, None, :], (CONV_WIDTH, SUBLANES, D_CONV)),
      bconv.reshape(1, -1), lng.reshape(1, -1), lnb.reshape(1, -1), wpool, pscale.reshape(1, -1), mix)


BLOCK_UNROLL = 16


def _softmax_sink_pv(s, sink_col, vv):
    m = jnp.maximum(jnp.max(s, axis=-1, keepdims=True), sink_col)
    e = jnp.exp(s - m)
    den = jnp.sum(e, axis=-1, keepdims=True) + jnp.exp(sink_col - m)
    return jnp.dot(e.astype(BF16), vv, preferred_element_type=F32) / den


def _attn_prompt_kernel(sink_ref, q_ref, k_ref, v_ref, mix_in_ref, o_ref, knew_ref, vnew_ref, kbuf, vbuf, bias,
                        *, seq):
    del mix_in_ref
    kh = pl.program_id(1)
    blk = ATTN_BLOCK
    n_blocks = seq // blk
    rows_all = Q_PER_KV * blk

    kbuf[0:blk, :] = jnp.zeros((blk, HEAD_DIM), BF16)
    vbuf[0:blk, :] = jnp.zeros((blk, HEAD_DIM), BF16)

    row = lax.broadcasted_iota(jnp.int32, (rows_all, 1), 0)
    qi = jnp.bitwise_and(row, blk - 1)
    kj = lax.broadcasted_iota(jnp.int32, (1, 2 * blk), 1)
    band = (kj >= qi) & (kj <= qi + WINDOW)
    bias[1] = jnp.where(band, 0.0, -jnp.inf)
    bias[0] = jnp.where(band & (kj >= blk), 0.0, -jnp.inf)
    sink_col = jnp.zeros((rows_all, 1), F32)
    for g in range(Q_PER_KV):
        sink_col = jnp.where((row >= g * blk) & (row < (g + 1) * blk), sink_ref[kh * Q_PER_KV + g], sink_col)
    scale = HEAD_DIM ** -0.5

    last = slice(seq - KV_BUF, seq)
    knew_ref[0] = k_ref[last, :]
    vnew_ref[0] = v_ref[last, :]
    kbuf[blk:blk + seq, :] = k_ref[...].astype(BF16)
    vbuf[blk:blk + seq, :] = v_ref[...].astype(BF16)

    def block(n, carry):
        r0 = pl.multiple_of(n * blk, blk)
        rows = pl.ds(r0, blk)
        q4 = jnp.concatenate(
            [q_ref[rows, g * HEAD_DIM:(g + 1) * HEAD_DIM].astype(BF16) for g in range(Q_PER_KV)], axis=0)
        kk = kbuf[pl.ds(r0, 2 * blk), :]
        vv = vbuf[pl.ds(r0, 2 * blk), :]
        s = lax.dot_general(q4, kk, (((1,), (1,)), ((), ())), preferred_element_type=F32) * scale
        s = s + bias[jnp.minimum(n, 1)]
        o = _softmax_sink_pv(s, sink_col, vv)
        for g in range(Q_PER_KV):
            o_ref[rows, g * HEAD_DIM:(g + 1) * HEAD_DIM] = o[g * blk:(g + 1) * blk, :].astype(o_ref.dtype)
        return carry

    lax.fori_loop(0, n_blocks, block, 0, unroll=BLOCK_UNROLL)


def _attn_prompt(proj, mix, sinks, *, batch, seq):
    m = proj.shape[0]
    gw = Q_PER_KV * HEAD_DIM
    return pl.pallas_call(
        functools.partial(_attn_prompt_kernel, seq=seq),
        grid=(batch, N_KV_HEADS),
        in_specs=[pl.BlockSpec(memory_space=pltpu.SMEM),
                  pl.BlockSpec((seq, gw), lambda b, h: (b, OFF_Q // gw + h)),
                  pl.BlockSpec((seq, HEAD_DIM), lambda b, h: (b, OFF_K // HEAD_DIM + h)),
                  pl.BlockSpec((seq, HEAD_DIM), lambda b, h: (b, OFF_V // HEAD_DIM + h)),
                  pl.BlockSpec(memory_space=pl.ANY)],
        out_specs=[pl.BlockSpec((seq, gw), lambda b, h: (b, (D_CONV + D_POOL) // gw + h)),
                   pl.BlockSpec((1, KV_BUF, HEAD_DIM), lambda b, h: (b, 0, h)),
                   pl.BlockSpec((1, KV_BUF, HEAD_DIM), lambda b, h: (b, 0, h))],
        out_shape=[jax.ShapeDtypeStruct((m, D_MIX), BF16),
                   jax.ShapeDtypeStruct((batch, KV_BUF, D_KV), F32),
                   jax.ShapeDtypeStruct((batch, KV_BUF, D_KV), F32)],
        scratch_shapes=[pltpu.VMEM((seq + ATTN_BLOCK, HEAD_DIM), BF16),
                        pltpu.VMEM((seq + ATTN_BLOCK, HEAD_DIM), BF16),
                        pltpu.VMEM((2, Q_PER_KV * ATTN_BLOCK, 2 * ATTN_BLOCK), F32)],
        input_output_aliases={4: 0},
        compiler_params=pltpu.CompilerParams(dimension_semantics=("arbitrary", "arbitrary")),
        name="attn_prompt",
    )(sinks, proj, proj, proj, mix)


KEYS_PAD = 256


def _sample_one(bb, row0, sink_ref, proj_ref, chist_ref, phist_ref, ck_ref, cv_ref, wconv_ref, bconv_ref,
                lng_ref, lnb_ref, wpool_ref, pscale_ref,
                cnew_ref, pnew_ref, knew_ref, vnew_ref, fbuf, pbuf, kkbuf, vvbuf, qbuf, obuf, mstep, ts):
    out_rows = slice(row0, row0 + ts)
    a = proj_ref[bb, :, OFF_A:OFF_A + D_CONV]
    b = proj_ref[bb, :, OFF_B:OFF_B + D_CONV]
    fbuf[0:CONV_BUF, :] = chist_ref[bb]
    fbuf[CONV_BUF:CONV_BUF + ts, :] = a * _sigmoid(b)
    acc = fbuf[0:ts, :] * wconv_ref[0:1, :]
    for k in range(1, CONV_WIDTH):
        acc = acc + fbuf[k:k + ts, :] * wconv_ref[k:k + 1, :]
    mstep[out_rows, 0:D_CONV] = _layernorm_silu(acc + bconv_ref[...], lng_ref[...], lnb_ref[...])
    cnew_ref[bb] = fbuf[ts:ts + CONV_BUF, :]

    pbuf[0:POOL_BUF, :] = phist_ref[bb]
    pbuf[POOL_BUF:POOL_BUF + ts, :] = proj_ref[bb, :, OFF_Z:OFF_Z + D_POOL]
    for g, w in enumerate(POOL_WINDOWS):
        y = _pool_group(pbuf, POOL_BUF, ts, g, w, PAST_LEN, wpool_ref, pscale_ref)
        mstep[out_rows, D_CONV + g * POOL_GROUP:D_CONV + (g + 1) * POOL_GROUP] = y
    pnew_ref[bb] = pbuf[ts:ts + POOL_BUF, :]

    for h in range(N_KV_HEADS):
        hs = slice(h * HEAD_DIM, (h + 1) * HEAD_DIM)
        kkbuf[0:KV_BUF, hs] = ck_ref[bb, :, h, :]
        vvbuf[0:KV_BUF, hs] = cv_ref[bb, :, h, :]
    kkbuf[KV_BUF:KV_BUF + ts, :] = proj_ref[bb, :, OFF_K:OFF_K + D_KV]
    vvbuf[KV_BUF:KV_BUF + ts, :] = proj_ref[bb, :, OFF_V:OFF_V + D_KV]
    pad = KEYS_PAD - KV_BUF - ts
    kkbuf[KV_BUF + ts:KEYS_PAD, :] = jnp.zeros((pad, D_KV), F32)
    vvbuf[KV_BUF + ts:KEYS_PAD, :] = jnp.zeros((pad, D_KV), F32)
    for h in range(N_KV_HEADS):
        hs = slice(h * HEAD_DIM, (h + 1) * HEAD_DIM)
        knew_ref[bb, :, h, :] = kkbuf[ts:ts + KV_BUF, hs]
        vnew_ref[bb, :, h, :] = vvbuf[ts:ts + KV_BUF, hs]

    rows_all = Q_PER_KV * ts
    row = lax.broadcasted_iota(jnp.int32, (rows_all, 1), 0)
    tq = row % ts
    kj = lax.broadcasted_iota(jnp.int32, (1, KEYS_PAD), 1)
    valid = (kj >= tq + (KV_BUF - WINDOW)) & (kj <= tq + KV_BUF) & (kj < KV_BUF + ts)
    scale = HEAD_DIM ** -0.5
    for kh in range(N_KV_HEADS):
        hs = slice(kh * HEAD_DIM, (kh + 1) * HEAD_DIM)
        sink_col = jnp.zeros((rows_all, 1), F32)
        for g in range(Q_PER_KV):
            h = kh * Q_PER_KV + g
            qbuf[g * ts:(g + 1) * ts, :] = proj_ref[bb, :, OFF_Q + h * HEAD_DIM:OFF_Q + (h + 1) * HEAD_DIM]
            sink_col = jnp.where((row >= g * ts) & (row < (g + 1) * ts), sink_ref[h], sink_col)
        q = qbuf[...].astype(BF16)
        kk = kkbuf[:, hs].astype(BF16)
        vv = vvbuf[:, hs].astype(BF16)
        s = lax.dot_general(q, kk, (((1,), (1,)), ((), ())), preferred_element_type=F32) * scale
        s = jnp.where(valid, s, -jnp.inf)
        obuf[...] = _softmax_sink_pv(s, sink_col, vv)
        for g in range(Q_PER_KV):
            h = kh * Q_PER_KV + g
            col = D_CONV + D_POOL + h * HEAD_DIM
            mstep[out_rows, col:col + HEAD_DIM] = obuf[g * ts:(g + 1) * ts, :]


def _sample_kernel(*refs, steps, seqs, n_steps, n_alias):
    (sink_ref, proj_ref, chist_ref, phist_ref, ck_ref, cv_ref, wconv_ref, bconv_ref,
     lng_ref, lnb_ref, wpool_ref, pscale_ref) = refs[:12]
    (mix_ref, cnew_ref, pnew_ref, knew_ref, vnew_ref, fbuf, pbuf, kkbuf, vvbuf, qbuf, obuf,
     mstep, mixacc) = refs[12 + n_alias:]
    step = pl.program_id(0)
    for bb in range(seqs):
        _sample_one(bb, bb * steps, sink_ref, proj_ref, chist_ref, phist_ref, ck_ref, cv_ref, wconv_ref,
                    bconv_ref, lng_ref, lnb_ref, wpool_ref, pscale_ref,
                    cnew_ref, pnew_ref, knew_ref, vnew_ref, fbuf.at[bb], pbuf.at[bb], kkbuf.at[bb], vvbuf.at[bb],
                    qbuf.at[bb], obuf.at[bb], mstep, steps)
    rows = seqs * steps
    mixacc[pl.ds(pl.multiple_of(step * rows, rows), rows), :] = mstep[...]

    @pl.when(step == n_steps - 1)
    def _():
        mix_ref[...] = mixacc[...].astype(mix_ref.dtype)


def _sample_mixers(proj, mix, chist, phist, ck, cv, sinks, wconv, bconv, lng, lnb, wpool, pscale,
                   *, row0, layer, states):
    nb, ts, _ = proj.shape
    depth = ck.shape[0]
    assert SUBLANES % ts == 0
    seqs = SUBLANES // ts
    assert nb % seqs == 0 and row0 % (nb * ts) == 0 and (nb * ts) % SUBLANES_BF16 == 0
    n_steps = nb // seqs
    per_step = lambda s: (s, 0, 0)
    const2 = lambda s: (0, 0)
    vec = lambda n: pl.BlockSpec((1, n), const2)
    stacked = lambda s: (layer, s, 0, 0)
    stacked_kv = lambda s: (layer, s, 0, 0, 0)
    kv_block = (None, seqs, KV_BUF, N_KV_HEADS, HEAD_DIM)
    aliased = [mix] + list(states)
    n_fixed = 12
    return pl.pallas_call(
        functools.partial(_sample_kernel, steps=ts, seqs=seqs, n_steps=n_steps, n_alias=len(aliased)),
        grid=(n_steps,),
        in_specs=[pl.BlockSpec(memory_space=pltpu.SMEM),
                  pl.BlockSpec((seqs, ts, D_IN), per_step),
                  pl.BlockSpec((None, seqs, CONV_BUF, D_CONV), stacked),
                  pl.BlockSpec((None, seqs, POOL_BUF, D_POOL), stacked),
                  pl.BlockSpec(kv_block, stacked_kv),
                  pl.BlockSpec(kv_block, stacked_kv),
                  pl.BlockSpec((CONV_WIDTH, D_CONV), const2),
                  vec(D_CONV), vec(D_CONV), vec(D_CONV),
                  pl.BlockSpec((len(POOL_WINDOWS), POOL_GROUP, POOL_GROUP), lambda s: (0, 0, 0)),
                  vec(D_POOL)] + [pl.BlockSpec(memory_space=pl.ANY)] * len(aliased),
        out_specs=[pl.BlockSpec((nb * ts, D_MIX), lambda s: (row0 // (nb * ts), 0)),
                   pl.BlockSpec((None, seqs, CONV_BUF, D_CONV), stacked),
                   pl.BlockSpec((None, seqs, POOL_BUF, D_POOL), stacked),
                   pl.BlockSpec(kv_block, stacked_kv),
                   pl.BlockSpec(kv_block, stacked_kv)],
        out_shape=[jax.ShapeDtypeStruct(mix.shape, mix.dtype),
                   jax.ShapeDtypeStruct((depth, nb, CONV_BUF, D_CONV), F32),
                   jax.ShapeDtypeStruct((depth, nb, POOL_BUF, D_POOL), F32),
                   jax.ShapeDtypeStruct((depth, nb, KV_BUF, N_KV_HEADS, HEAD_DIM), F32),
                   jax.ShapeDtypeStruct((depth, nb, KV_BUF, N_KV_HEADS, HEAD_DIM), F32)],
        scratch_shapes=[pltpu.VMEM((seqs, CONV_BUF + 2 * ts + 2, D_CONV), F32),
                        pltpu.VMEM((seqs, POOL_BUF + 2 * ts + 1, D_POOL), F32),
                        pltpu.VMEM((seqs, KEYS_PAD, D_KV), F32),
                        pltpu.VMEM((seqs, KEYS_PAD, D_KV), F32),
                        pltpu.VMEM((seqs, Q_PER_KV * ts, HEAD_DIM), F32),
                        pltpu.VMEM((seqs, Q_PER_KV * ts, HEAD_DIM), F32),
                        pltpu.VMEM((seqs * ts, D_MIX), F32),
                        pltpu.VMEM((nb * ts, D_MIX), F32)],
        input_output_aliases={n_fixed + a: a for a in range(len(aliased))},
        compiler_params=pltpu.CompilerParams(dimension_semantics=("arbitrary",)),
        name="sample_mixers",
    )(sinks, proj, chist, phist, ck, cv, wconv, bconv.reshape(1, -1), lng.reshape(1, -1), lnb.reshape(1, -1),
      wpool, pscale.reshape(1, -1), *aliased)


def kernel(x_prompt, x_sample, state_conv, state_pool, cache_k, cache_v, norm_mix, w_in, w_conv, b_conv,
           ln_conv_g, ln_conv_b, w_pool, pool_scale, q_norm, k_norm, attn_sinks, w_out, norm_ffn, w_up, w_down):
    bp, sp, d_model = x_prompt.shape
    bs, ts, _ = x_sample.shape
    depth = w_in.shape[0]
    assert w_in.shape[2] == D_IN and sp % ATTN_BLOCK == 0 and KV_BUF == WINDOW
    mp, ms = bp * sp, bs * ts

    tab_p = _rope_tables(jnp.arange(sp, dtype=jnp.int32))
    tab_s = _rope_tables(PAST_LEN + jnp.arange(ts, dtype=jnp.int32))
    tables = tuple(jnp.concatenate([jnp.tile(tp, (bp, 1)), jnp.tile(tsm, (bs, 1))], axis=0)
                   for tp, tsm in zip(tab_p, tab_s))
    head_mask = jnp.zeros((D_IN,), F32).at[OFF_Q:OFF_V].set(1.0)
    zero_conv = jnp.zeros((bp, CONV_BUF, D_CONV), F32)
    zero_pool = jnp.zeros((bp, POOL_BUF, D_POOL), F32)
    n_chunks = w_up.shape[2] // d_model

    mix = jnp.zeros((mp + ms, D_MIX), BF16)
    states_s = [jnp.zeros(a.shape, F32) for a in (state_conv, state_pool, cache_k, cache_v)]

    outs = [[] for _ in range(4)]
    x, xg, ssq = _merge_prenorm(x_prompt.reshape(mp, d_model), x_sample.reshape(ms, d_model), norm_mix[0])
    for l in range(depth):
        mixer_w = (w_conv[l], b_conv[l], ln_conv_g[l], ln_conv_b[l], w_pool[l], pool_scale[l])
        head_gain = jnp.concatenate([jnp.ones((OFF_Q,), F32), jnp.tile(q_norm[l], N_HEADS),
                                     jnp.tile(k_norm[l], N_KV_HEADS), jnp.ones((D_KV,), F32)])

        proj = _wsmm(xg, w_in, l, ssq=ssq, rope=(head_gain, head_mask) + tables, tm_target=ROPE_TM, n_sub=ROPE_SUB,
                     name="in_proj")
        mix, conv_p, pool_p = _convpool_prompt(proj, mix, zero_conv, zero_pool, *mixer_w, batch=bp, seq=sp)
        mix, k_p, v_p = _attn_prompt(proj, mix, attn_sinks[l], batch=bp, seq=sp)
        mix, *states_s = _sample_mixers(
            proj[mp:].reshape(bs, ts, D_IN), mix, state_conv, state_pool, cache_k, cache_v,
            attn_sinks[l], *mixer_w, row0=mp, layer=l, states=states_s)
        x, xg, ssq = _wsmm(mix, w_out, l, res=x, norm_gain=norm_ffn[l], tm_target=NORM_TM, name="out_proj")

        up = _wsmm(xg, w_up, l, ssq=ssq, relu2=True, out_dtype=BF16, name="ffn_up")
        for c in range(n_chunks):
            if c == n_chunks - 1 and l + 1 < depth:
                x, xg, ssq = _wsmm(up, w_down, l, k_chunk=c, res=x, norm_gain=norm_mix[l + 1], tm_target=NORM_TM,
                                   name="ffn_down")
            else:
                last = c == n_chunks - 1
                x = _wsmm(up, w_down, l, k_chunk=c, res=x, tm_target=RES_TM, split_rows=mp if last else None,
                          name="ffn_down")

        for lst, val in zip(outs, (conv_p, pool_p, k_p, v_p)):
            lst.append(val)

    conv_p, pool_p, k_p, v_p = [jnp.stack(o) for o in outs]
    conv_s, pool_s, k_s, v_s = states_s
    kv_shape_p = (depth, bp, KV_BUF, N_KV_HEADS, HEAD_DIM)
    y_prompt, y_sample = x
    return (y_prompt.reshape(bp, sp, d_model), y_sample.reshape(bs, ts, d_model),
            conv_p, pool_p, k_p.reshape(kv_shape_p), v_p.reshape(kv_shape_p), conv_s, pool_s, k_s, v_s)
```
